```python
import math
import functools
import numpy as np
import jax
import jax.numpy as jnp
from jax import lax

D_MODEL = 2048
BATCH = 8
SEQ = 2048
DEPTH = 1
DEC_BATCH = 32
DEC_SEQ = 1
PAST_LEN = 8192
PAGE_SIZE = 128

N_HEADS = 8
HEAD_DIM = 128
V_DIM = 2 * HEAD_DIM
QK_WIDTH = N_HEADS * 2 * HEAD_DIM
ATTN_WIDTH = N_HEADS * V_DIM
CONV_DIM = D_MODEL
CONV_W = 31
N_GROUPS = 4
EXPERTS_PER_GROUP = 8
N_EXPERTS = N_GROUPS * EXPERTS_PER_GROUP
TOP_K = 2
D_EXPERT = D_MODEL // 2
N_MOD = 6
Q_BLOCK = 128
MOE_BLOCK = 128
EPS = 1e-6
HEAD_EPS = 1e-5
N_IN = 2 * QK_WIDTH + ATTN_WIDTH + 2 * CONV_DIM + 2 * D_MODEL

kernel_name = 'hybrid_diffattn_conformer_hmoe_step'


def rms_norm(x, g):
    xf = x.astype(jnp.float32)
    y = xf * lax.rsqrt(jnp.mean(xf * xf, axis=-1, keepdims=True) + EPS)
    return (y * g.astype(jnp.float32)).astype(x.dtype)


def head_rms_norm(o, g):
    of = o.astype(jnp.float32)
    y = of * lax.rsqrt(jnp.mean(of * of, axis=-1, keepdims=True) + HEAD_EPS)
    return (y * g.astype(jnp.float32)).astype(o.dtype)


def layer_norm(x, g, b):
    xf = x.astype(jnp.float32)
    xc = xf - jnp.mean(xf, axis=-1, keepdims=True)
    y = xc * lax.rsqrt(jnp.mean(xc * xc, axis=-1, keepdims=True) + EPS)
    return (y * g.astype(jnp.float32) + b.astype(jnp.float32)).astype(x.dtype)


def projection_splits():
    sizes = (QK_WIDTH, QK_WIDTH, ATTN_WIDTH, CONV_DIM, CONV_DIM, D_MODEL)
    return [int(s) for s in np.cumsum(sizes)]


def diff_attn_prompt(q, k, v, lam):
    b, t = q.shape[:2]
    nb = t // Q_BLOCK
    scale = HEAD_DIM ** -0.5
    qb = jnp.moveaxis(q.reshape(b, nb, Q_BLOCK, N_HEADS, 2, HEAD_DIM), 1, 0)
    k_pos = jnp.arange(t)

    def block(args):
        q_blk, blk = args
        s = jnp.einsum('bqhmd,bkhmd->bhmqk', q_blk, k).astype(jnp.float32) * scale
        q_pos = blk * Q_BLOCK + jnp.arange(Q_BLOCK)
        s = jnp.where(k_pos[None, :] <= q_pos[:, None], s, -jnp.inf)
        p = jax.nn.softmax(s, axis=-1)
        w = (p[:, :, 0] - lam * p[:, :, 1]).astype(v.dtype)
        return jnp.einsum('bhqk,bkhd->bqhd', w, v)

    o = lax.map(block, (qb, jnp.arange(nb)))
    return jnp.moveaxis(o, 0, 1).reshape(b, t, N_HEADS, V_DIM)


def diff_attn_decode(q, k, v, lam, k_past_pages, v_past_pages):
    b, t = q.shape[:2]
    scale = HEAD_DIM ** -0.5
    k_past = k_past_pages.reshape(b, -1, N_HEADS, 2, HEAD_DIM)
    v_past = v_past_pages.reshape(b, -1, N_HEADS, V_DIM)
    n_past = k_past.shape[1]
    s_past = jnp.einsum('bqhmd,bkhmd->bhmqk', q, k_past).astype(jnp.float32) * scale
    s_new = jnp.einsum('bqhmd,bkhmd->bhmqk', q, k).astype(jnp.float32) * scale
    s_new = jnp.where(jnp.tril(jnp.ones((t, t), dtype=bool)), s_new, -jnp.inf)
    p = jax.nn.softmax(jnp.concatenate([s_past, s_new], axis=-1), axis=-1)
    w = (p[:, :, 0] - lam * p[:, :, 1]).astype(v.dtype)
    return (jnp.einsum('bhqk,bkhd->bqhd', w[..., :n_past], v_past)
            + jnp.einsum('bhqk,bkhd->bqhd', w[..., n_past:], v))


def causal_depthwise_conv(hist, u, w_dw, b_dw):
    full = jnp.concatenate([hist, u], axis=1)
    y = lax.conv_general_dilated(full, w_dw[:, None, :], window_strides=(1,), padding='VALID',
                                 dimension_numbers=('NWC', 'WIO', 'NWC'),
                                 feature_group_count=CONV_DIM)
    return y + b_dw, full[:, -(CONV_W - 1):]


def swiglu_expert(xb, wg, wu, wd):
    return (jax.nn.silu(xb @ wg) * (xb @ wu)) @ wd


def hier_moe(h, w_gr, b_gr, w_er, b_er, w_g, w_u, w_d):
    T, D = h.shape
    g_logits = (h @ w_gr).astype(jnp.float32) + b_gr.astype(jnp.float32)
    grp = jnp.argmax(g_logits, axis=-1)
    p_grp = jnp.take_along_axis(jax.nn.softmax(g_logits, axis=-1), grp[:, None], axis=-1)
    e_logits = ((h @ w_er).astype(jnp.float32) + b_er.astype(jnp.float32)).reshape(T, N_GROUPS, EXPERTS_PER_GROUP)
    e_logits = jnp.take_along_axis(e_logits, grp[:, None, None], axis=1)[:, 0]
    top_val, top_idx = lax.top_k(e_logits, TOP_K)
    weight = p_grp * jax.nn.softmax(top_val, axis=-1)
    expert = grp[:, None] * EXPERTS_PER_GROUP + top_idx

    P = T * TOP_K
    flat_e = expert.reshape(P)
    order = jnp.argsort(flat_e)
    sorted_e = flat_e[order]
    counts = jnp.zeros((N_EXPERTS,), jnp.int32).at[flat_e].add(1)
    padded = (counts + MOE_BLOCK - 1) // MOE_BLOCK * MOE_BLOCK
    pad_end = jnp.cumsum(padded)
    pad_start = pad_end - padded
    start = jnp.cumsum(counts) - counts
    dest = pad_start[sorted_e] + jnp.arange(P) - start[sorted_e]
    n_blocks = -(-P // MOE_BLOCK) + N_EXPERTS
    slot_token = jnp.zeros((n_blocks * MOE_BLOCK,), jnp.int32).at[dest].set(order // TOP_K)
    block_expert = jnp.minimum(
        jnp.searchsorted(pad_end, jnp.arange(n_blocks) * MOE_BLOCK, side='right'), N_EXPERTS - 1)
    xs = h[slot_token].reshape(n_blocks, MOE_BLOCK, D)

    def run(args):
        xb, e = args
        return swiglu_expert(xb, w_g[e], w_u[e], w_d[e])

    ys = lax.map(run, (xs, block_expert)).reshape(n_blocks * MOE_BLOCK, D)
    y_pair = ys[dest] * weight.reshape(P)[order][:, None].astype(ys.dtype)
    return jnp.zeros((T, D), ys.dtype).at[order // TOP_K].add(y_pair)


def decoder_layer(x, c, lp, lam_init, attend, hist):
    b, t, _ = x.shape
    mod = jax.nn.silu(c) @ lp['w_ada'] + lp['b_ada']
    sh1, sc1, g1, sh2, sc2, g2 = jnp.split(mod[:, None, :], N_MOD, axis=-1)

    h = rms_norm(x, lp['g_mix']) * (1 + sc1) + sh1
    proj = h @ lp['w_in']
    q, k, v, u_val, u_gate, a_gate, c_gate = jnp.split(proj, projection_splits(), axis=-1)
    q = q.reshape(b, t, N_HEADS, 2, HEAD_DIM)
    k = k.reshape(b, t, N_HEADS, 2, HEAD_DIM)
    v = v.reshape(b, t, N_HEADS, V_DIM)
    lam = (jnp.exp(jnp.sum((lp['lq1'] * lp['lk1']).astype(jnp.float32)))
           - jnp.exp(jnp.sum((lp['lq2'] * lp['lk2']).astype(jnp.float32))) + lam_init)
    o = attend(q, k, v, lam)
    o = head_rms_norm(o, lp['g_head']) * (1 - lam_init)
    y_attn = o.reshape(b, t, ATTN_WIDTH) @ lp['w_attn_proj']

    u = u_val * jax.nn.sigmoid(u_gate)
    uc, new_hist = causal_depthwise_conv(hist, u, lp['w_dw'], lp['b_dw'])
    y_conv = jax.nn.silu(layer_norm(uc, lp['g_cln'], lp['b_cln'])) @ lp['w_conv_proj']

    merged = jax.nn.sigmoid(a_gate) * y_attn + jax.nn.sigmoid(c_gate) * y_conv
    x = x + g1 * (merged @ lp['w_out'])

    h2 = rms_norm(x, lp['g_ffn']) * (1 + sc2) + sh2
    ffn = hier_moe(h2.reshape(b * t, D_MODEL), lp['w_gr'], lp['b_gr'], lp['w_er'], lp['b_er'],
                   lp['w_g'], lp['w_u'], lp['w_d']).reshape(b, t, D_MODEL)
    x = x + g2 * ffn
    return x, k, v, new_hist


def setup_inputs(seed: int = 0) -> dict:
    key = jax.random.key(seed)
    keys = iter(jax.random.split(key, 40))

    def nrm(shape, scale):
        return jax.random.normal(next(keys), shape, jnp.float32) * scale

    def gain(shape):
        return 1.0 + nrm(shape, 0.02)

    d = D_MODEL
    n_pages = PAST_LEN // PAGE_SIZE
    n_used = DEC_BATCH * n_pages
    n_pool = n_used + max(1, n_used // 4)
    page_table = jax.random.permutation(next(keys), n_pool)[:n_used].reshape(DEC_BATCH, n_pages).astype(jnp.int32)
    return {
        'x_prompt': nrm((BATCH, SEQ, d), 1.0),
        'x_sample': nrm((DEC_BATCH, DEC_SEQ, d), 1.0),
        'cache_k': nrm((DEPTH, n_pool, PAGE_SIZE, N_HEADS, 2, HEAD_DIM), 1.0),
        'cache_v': nrm((DEPTH, n_pool, PAGE_SIZE, N_HEADS, V_DIM), 1.0),
        'state_conv': nrm((DEPTH, DEC_BATCH, CONV_W - 1, CONV_DIM), 0.5),
        'page_table': page_table,
        'c_prompt': nrm((BATCH, d), 1.0),
        'c_sample': nrm((DEC_BATCH, d), 1.0),
        'w_ada': nrm((DEPTH, d, N_MOD * d), 0.5 * d ** -0.5),
        'b_ada': nrm((DEPTH, N_MOD * d), 0.02),
        'g_mix': gain((DEPTH, d)),
        'w_in': nrm((DEPTH, d, N_IN), d ** -0.5),
        'lambda_q1': nrm((DEPTH, HEAD_DIM), 0.1),
        'lambda_k1': nrm((DEPTH, HEAD_DIM), 0.1),
        'lambda_q2': nrm((DEPTH, HEAD_DIM), 0.1),
        'lambda_k2': nrm((DEPTH, HEAD_DIM), 0.1),
        'g_head': gain((DEPTH, V_DIM)),
        'w_attn_proj': nrm((DEPTH, ATTN_WIDTH, d), ATTN_WIDTH ** -0.5),
        'w_dw': nrm((DEPTH, CONV_W, CONV_DIM), CONV_W ** -0.5),
        'b_dw': nrm((DEPTH, CONV_DIM), 0.02),
        'g_conv_ln': gain((DEPTH, CONV_DIM)),
        'b_conv_ln': nrm((DEPTH, CONV_DIM), 0.02),
        'w_conv_proj': nrm((DEPTH, CONV_DIM, d), CONV_DIM ** -0.5),
        'w_out': nrm((DEPTH, d, d), d ** -0.5),
        'g_ffn': gain((DEPTH, d)),
        'w_group_router': nrm((DEPTH, d, N_GROUPS), d ** -0.5),
        'b_group_router': nrm((DEPTH, N_GROUPS), 0.01),
        'w_expert_router': nrm((DEPTH, d, N_EXPERTS), d ** -0.5),
        'b_expert_router': nrm((DEPTH, N_EXPERTS), 0.01),
        'w_gate_e': nrm((DEPTH, N_EXPERTS, d, D_EXPERT), d ** -0.5),
        'w_up_e': nrm((DEPTH, N_EXPERTS, d, D_EXPERT), d ** -0.5),
        'w_down_e': nrm((DEPTH, N_EXPERTS, D_EXPERT, d), D_EXPERT ** -0.5),
        'g_final': gain((d,)),
    }


def reference(x_prompt, x_sample, cache_k, cache_v, state_conv, page_table, c_prompt, c_sample,
              w_ada, b_ada, g_mix, w_in, lambda_q1, lambda_k1, lambda_q2, lambda_k2, g_head,
              w_attn_proj, w_dw, b_dw, g_conv_ln, b_conv_ln, w_conv_proj, w_out, g_ffn,
              w_group_router, b_group_router, w_expert_router, b_expert_router,
              w_gate_e, w_up_e, w_down_e, g_final):
    xp, xs = x_prompt, x_sample
    kp_rows, vp_rows, hp_rows, ks_rows, vs_rows, hs_rows = [], [], [], [], [], []
    for l in range(DEPTH):
        lp = dict(w_ada=w_ada[l], b_ada=b_ada[l], g_mix=g_mix[l], w_in=w_in[l],
                  lq1=lambda_q1[l], lk1=lambda_k1[l], lq2=lambda_q2[l], lk2=lambda_k2[l],
                  g_head=g_head[l], w_attn_proj=w_attn_proj[l], w_dw=w_dw[l], b_dw=b_dw[l],
                  g_cln=g_conv_ln[l], b_cln=b_conv_ln[l], w_conv_proj=w_conv_proj[l],
                  w_out=w_out[l], g_ffn=g_ffn[l], w_gr=w_group_router[l], b_gr=b_group_router[l],
                  w_er=w_expert_router[l], b_er=b_expert_router[l],
                  w_g=w_gate_e[l], w_u=w_up_e[l], w_d=w_down_e[l])
        lam_init = 0.8 - 0.6 * math.exp(-0.3 * l)

        hist0 = jnp.zeros((xp.shape[0], CONV_W - 1, CONV_DIM), xp.dtype)
        xp, k_new, v_new, h_new = decoder_layer(xp, c_prompt, lp, lam_init, diff_attn_prompt, hist0)
        kp_rows.append(k_new)
        vp_rows.append(v_new)
        hp_rows.append(h_new)

        attend_s = functools.partial(diff_attn_decode,
                                     k_past_pages=cache_k[l, page_table],
                                     v_past_pages=cache_v[l, page_table])
        xs, k_new, v_new, h_new = decoder_layer(xs, c_sample, lp, lam_init, attend_s, state_conv[l])
        ks_rows.append(k_new)
        vs_rows.append(v_new)
        hs_rows.append(h_new)

    y_prompt = rms_norm(xp, g_final)
    y_sample = rms_norm(xs, g_final)
    return (y_prompt, y_sample, jnp.stack(kp_rows), jnp.stack(vp_rows), jnp.stack(hp_rows),
            jnp.stack(ks_rows), jnp.stack(vs_rows), jnp.stack(hs_rows))
```

```python
import functools
import math

import jax
import jax.numpy as jnp
from jax import lax
from jax.experimental import pallas as pl
from jax.experimental.pallas import tpu as pltpu

F32 = jnp.float32
BF16 = jnp.bfloat16
I32 = jnp.int32

D = 2048
H = 8
HD = 128
VD = 2 * HD
N_MAPS = 2 * H
CONV_W = 31
HIST = CONV_W - 1
N_GROUPS = 4
EPG = 8
N_EXPERTS = N_GROUPS * EPG
D_EXPERT = D // 2
N_MOD = 6
N_IN = 7 * D
EPS = 1e-6
HEAD_EPS = 1e-5
LAM_INIT = 0.8 - 0.6 * math.exp(0.0)
NEG = -1e30
LANES = 128
MIB = 1024 * 1024

MOE_ROWS = 256
TOK_TILE = 256
COMB_TILE = 128


def _params(sem, vmem_mb):
    return pltpu.CompilerParams(dimension_semantics=sem, vmem_limit_bytes=vmem_mb * MIB)


def _dot(a, b):
    return jnp.dot(a, b, preferred_element_type=F32)


def _split(x):
    hi = x.astype(BF16)
    lo = (x - hi.astype(F32)).astype(BF16)
    return hi, lo


def _dot3(a, b):
    ah, al = _split(a)
    bh, bl = _split(b)
    return _dot(ah, bh) + (_dot(al, bh) + _dot(ah, bl))


def _sigmoid(x):
    return 1.0 / (1.0 + jnp.exp(-x))


def _rms(x, g):
    return x * lax.rsqrt(jnp.mean(x * x, axis=-1, keepdims=True) + EPS) * g


def _mm_body(*refs, n_lhs, n_w, n_ex, pairs, prologue, epilogue, three_pass):
    lhs = [r[...] for r in refs[:n_lhs]]
    w_refs = refs[n_lhs:n_lhs + n_w]
    ex = [r[...] for r in refs[n_lhs + n_w:n_lhs + n_w + n_ex]]
    outs = refs[n_lhs + n_w + n_ex:]
    if prologue is not None:
        lhs = prologue(lhs, ex)
    accs = []
    for li, wi in pairs:
        b = w_refs[wi][...]
        accs.append(_dot3(lhs[li], b) if three_pass else _dot(lhs[li], b))
    for r, o in zip(outs, epilogue(accs, ex)):
        r[...] = o.astype(r.dtype)


def _mm_call(lhs, ws, pairs, extras, epilogue, out_dtypes, *, name, tm, tn, n_col, three_pass=False, prologue=None,
             vmem_mb=48):
    m = lhs[0].shape[0]
    in_specs = [pl.BlockSpec((tm, a.shape[1]), lambda i, j: (i, 0)) for a in lhs]
    in_specs += [pl.BlockSpec((w.shape[0], tn), functools.partial(lambda i, j, off: (0, j + off), off=off))
                 for w, off in ws]
    in_specs += [pl.BlockSpec(blk, imap) for _, blk, imap in extras]
    body = functools.partial(_mm_body, n_lhs=len(lhs), n_w=len(ws), n_ex=len(extras), pairs=pairs,
                             prologue=prologue, epilogue=epilogue, three_pass=three_pass)
    outs = pl.pallas_call(
        body,
        grid=(m // tm, n_col),
        in_specs=in_specs,
        out_specs=[pl.BlockSpec((tm, tn), lambda i, j: (i, j)) for _ in out_dtypes],
        out_shape=[jax.ShapeDtypeStruct((m, n_col * tn), dt) for dt in out_dtypes],
        compiler_params=_params(("parallel", "arbitrary"), vmem_mb),
        name=name,
    )(*lhs, *[w for w, _ in ws], *[a for a, _, _ in extras])
    return outs


def _normmod_body(x_ref, g_ref, sc_ref, sh_ref, o_ref):
    y = _rms(x_ref[...], g_ref[...])
    o_ref[...] = (y * (1.0 + sc_ref[0]) + sh_ref[0]).astype(o_ref.dtype)


def _normmod_call(x, g, sc, sh, *, tm, rows_per_mod, out_dtype):
    m = x.shape[0]
    r = sc.shape[1]
    mod_spec = pl.BlockSpec((1, r, D), lambda i: (i * tm // rows_per_mod, 0, 0))
    return pl.pallas_call(
        _normmod_body,
        grid=(m // tm,),
        in_specs=[pl.BlockSpec((tm, D), lambda i: (i, 0)), pl.BlockSpec((1, D), lambda i: (0, 0)), mod_spec, mod_spec],
        out_specs=pl.BlockSpec((tm, D), lambda i: (i, 0)),
        out_shape=jax.ShapeDtypeStruct((m, D), out_dtype),
        compiler_params=_params(("parallel",), 40),
        name="normmod",
    )(x, g, sc, sh)


def _attn_body(lam_ref, q_ref, k_ref, v_ref, gh_ref, o_ref, kb, vb, *, tq, scale):
    qi = pl.program_id(2)

    @pl.when(qi == 0)
    def _():
        kb[...] = k_ref[...].astype(BF16)
        vb[...] = v_ref[...].astype(BF16)

    q = q_ref[...]
    lam = lam_ref[0]
    nt = (((1,), (1,)), ((), ()))

    def step(kc, carry, masked):
        off = pl.multiple_of(kc * tq, tq)
        ks = kb[pl.ds(off, tq), :]
        vs = vb[pl.ds(off, tq), :]
        new = []
        for mp in range(2):
            m, l, a = carry[mp]
            s = lax.dot_general(q[:, mp * HD:(mp + 1) * HD], ks[:, mp * HD:(mp + 1) * HD], nt,
                                preferred_element_type=F32) * scale
            if masked:
                row = lax.broadcasted_iota(I32, (tq, tq), 0)
                col = lax.broadcasted_iota(I32, (tq, tq), 1)
                s = jnp.where(col <= row, s, NEG)
            mn = jnp.maximum(m, jnp.max(s, axis=1, keepdims=True))
            alpha = jnp.exp(m - mn)
            p = jnp.exp(s - mn)
            l = alpha * l + jnp.sum(p, axis=1, keepdims=True)
            a = alpha * a + _dot(p.astype(BF16), vs)
            new.append((mn, l, a))
        return tuple(new)

    init = tuple((jnp.full((tq, 1), NEG, F32), jnp.zeros((tq, 1), F32), jnp.zeros((tq, VD), F32)) for _ in range(2))
    carry = lax.fori_loop(0, qi, lambda kc, c: step(kc, c, False), init)
    (_, l0, a0), (_, l1, a1) = step(qi, carry, True)
    o = a0 / l0 - lam * (a1 / l1)
    o = o * lax.rsqrt(jnp.mean(o * o, axis=1, keepdims=True) + HEAD_EPS) * gh_ref[...] * (1.0 - LAM_INIT)
    o_ref[...] = o.astype(o_ref.dtype)


def _attn_call(lam, q, k, v, g_head, *, batch, seq, tq):
    nq = seq // tq
    body = functools.partial(_attn_body, tq=tq, scale=HD ** -0.5)
    return pl.pallas_call(
        body,
        grid=(batch, H, nq),
        in_specs=[
            pl.BlockSpec(memory_space=pltpu.SMEM),
            pl.BlockSpec((tq, VD), lambda b, h, i: (b * nq + i, h)),
            pl.BlockSpec((seq, VD), lambda b, h, i: (b, h)),
            pl.BlockSpec((seq, VD), lambda b, h, i: (b, h)),
            pl.BlockSpec((1, VD), lambda b, h, i: (0, 0)),
        ],
        out_specs=pl.BlockSpec((tq, VD), lambda b, h, i: (b * nq + i, h)),
        out_shape=jax.ShapeDtypeStruct((batch * seq, D), BF16),
        scratch_shapes=[pltpu.VMEM((seq, VD), BF16), pltpu.VMEM((seq, VD), BF16)],
        compiler_params=_params(("parallel", "parallel", "arbitrary"), 40),
        name="prompt_attn",
    )(lam, q, k, v, g_head)


def _decode_body(pt_ref, lam_ref, q_ref, kn_ref, vn_ref, gh_ref, *rest, n_pg, scale):
    k_refs = rest[:n_pg]
    v_refs = rest[n_pg:2 * n_pg]
    o_ref = rest[2 * n_pg]
    qb, m_s, l_s, acc_s = rest[2 * n_pg + 1:]
    j = pl.program_id(1)
    nt = (((1,), (1,)), ((), ()))

    @pl.when(j == 0)
    def _():
        row = lax.broadcasted_iota(I32, (N_MAPS, D), 0)
        col = lax.broadcasted_iota(I32, (N_MAPS, D), 1)
        qb[...] = jnp.where(col // HD == row, jnp.broadcast_to(q_ref[0], (N_MAPS, D)), 0.0).astype(BF16)
        m_s[...] = jnp.full((N_MAPS, 1), NEG, F32)
        l_s[...] = jnp.zeros((N_MAPS, 1), F32)
        acc_s[...] = jnp.zeros((N_MAPS, D), F32)

    qv = qb[...]
    s = jnp.concatenate(
        [lax.dot_general(qv, k_refs[g][0].astype(BF16), nt, preferred_element_type=F32) for g in range(n_pg)],
        axis=1) * scale
    m_old = m_s[...]
    m_new = jnp.maximum(m_old, jnp.max(s, axis=1, keepdims=True))
    alpha = jnp.exp(m_old - m_new)
    p = jnp.exp(s - m_new)
    l_s[...] = alpha * l_s[...] + jnp.sum(p, axis=1, keepdims=True)
    pb = p.astype(BF16)
    pv = _dot(pb[:, 0:LANES], v_refs[0][0].astype(BF16))
    for g in range(1, n_pg):
        pv = pv + _dot(pb[:, g * LANES:(g + 1) * LANES], v_refs[g][0].astype(BF16))
    acc_s[...] = alpha * acc_s[...] + pv
    m_s[...] = m_new

    @pl.when(j == pl.num_programs(1) - 1)
    def _():
        qf = qb[...].astype(F32)
        kn = kn_ref[0].astype(BF16).astype(F32)
        vn = vn_ref[0].astype(BF16).astype(F32)
        s_new = jnp.sum(qf * kn, axis=1, keepdims=True) * scale
        m_fin = jnp.maximum(m_s[...], s_new)
        a = jnp.exp(m_s[...] - m_fin)
        p_new = jnp.exp(s_new - m_fin)
        l_fin = l_s[...] * a + p_new
        accn = (acc_s[...] * a + p_new.astype(BF16).astype(F32) * vn) / l_fin
        lam = lam_ref[0]
        for h in range(H):
            cs = slice(h * VD, (h + 1) * VD)
            o = accn[2 * h:2 * h + 1, cs] - lam * accn[2 * h + 1:2 * h + 2, cs]
            o = o * lax.rsqrt(jnp.mean(o * o, axis=1, keepdims=True) + HEAD_EPS) * gh_ref[...] * (1.0 - LAM_INIT)
            o_ref[0, :, cs] = o


def _decode_call(page_table, lam, q, k_new, v_new, g_head, cache_k, cache_v, *, n_pg):
    b, n_pages = page_table.shape
    page = cache_k.shape[1]
    row_spec = pl.BlockSpec((1, 1, D), lambda bi, j, pt: (bi, 0, 0))
    page_specs = [pl.BlockSpec((1, page, D), functools.partial(lambda bi, j, pt, g: (pt[bi, j * n_pg + g], 0, 0), g=g))
                  for g in range(n_pg)]
    body = functools.partial(_decode_body, n_pg=n_pg, scale=HD ** -0.5)
    grid_spec = pltpu.PrefetchScalarGridSpec(
        num_scalar_prefetch=1,
        grid=(b, n_pages // n_pg),
        in_specs=[pl.BlockSpec(memory_space=pltpu.SMEM), row_spec, row_spec, row_spec,
                  pl.BlockSpec((1, VD), lambda bi, j, pt: (0, 0))] + page_specs + page_specs,
        out_specs=pl.BlockSpec((1, 1, D), lambda bi, j, pt: (bi, 0, 0)),
        scratch_shapes=[pltpu.VMEM((N_MAPS, D), BF16), pltpu.VMEM((N_MAPS, 1), F32), pltpu.VMEM((N_MAPS, 1), F32),
                        pltpu.VMEM((N_MAPS, D), F32)],
    )
    return pl.pallas_call(
        body,
        grid_spec=grid_spec,
        out_shape=jax.ShapeDtypeStruct((b, 1, D), F32),
        compiler_params=_params(("parallel", "arbitrary"), 56),
        name="decode_attn",
    )(page_table, lam, q, k_new, v_new, g_head, *([cache_k] * n_pg), *([cache_v] * n_pg))


CONV_HALO = 32
CONV_RC = 16
CONV_CB = 512


def _ln_swish(acc, g, b):
    xc = acc - jnp.mean(acc, axis=1, keepdims=True)
    y = xc * lax.rsqrt(jnp.mean(xc * xc, axis=1, keepdims=True) + EPS) * g + b
    return y * _sigmoid(y)


def _conv_body(cur_ref, prev_ref, w_ref, bdw_ref, g_ref, b_ref, z_ref, full, uc, *, tt):
    t = pl.program_id(1)
    full[0:CONV_HALO, :] = jnp.where(t > 0, prev_ref[0], 0.0)
    full[CONV_HALO:, :] = cur_ref[0]
    shift = CONV_HALO - HIST
    for c in range(tt // CONV_RC):
        r0 = c * CONV_RC
        for cb in range(D // CONV_CB):
            cs = slice(cb * CONV_CB, (cb + 1) * CONV_CB)
            acc = jnp.broadcast_to(bdw_ref[:, cs], (CONV_RC, CONV_CB))
            for j in range(CONV_W):
                acc = acc + full[r0 + shift + j:r0 + shift + j + CONV_RC, cs] * w_ref[j:j + 1, cs]
            uc[r0:r0 + CONV_RC, cs] = acc

    def norm(c, _):
        r0 = pl.multiple_of(c * CONV_RC, CONV_RC)
        z = _ln_swish(uc[pl.ds(r0, CONV_RC), :], g_ref[...], b_ref[...])
        z_ref[0, pl.ds(r0, CONV_RC), :] = z.astype(z_ref.dtype)
        return 0

    lax.fori_loop(0, tt // CONV_RC, norm, 0)


def _conv_call(u, w_dw, b_dw, g_cln, b_cln, *, tt):
    b, s, _ = u.shape
    ratio = tt // CONV_HALO
    vec = pl.BlockSpec((1, D), lambda bi, t: (0, 0))
    return pl.pallas_call(
        functools.partial(_conv_body, tt=tt),
        grid=(b, s // tt),
        in_specs=[pl.BlockSpec((1, tt, D), lambda bi, t: (bi, t, 0)),
                  pl.BlockSpec((1, CONV_HALO, D), lambda bi, t: (bi, jnp.maximum(t * ratio - 1, 0), 0)),
                  pl.BlockSpec((CONV_HALO, D), lambda bi, t: (0, 0)), vec, vec, vec],
        out_specs=pl.BlockSpec((1, tt, D), lambda bi, t: (bi, t, 0)),
        out_shape=jax.ShapeDtypeStruct((b, s, D), BF16),
        scratch_shapes=[pltpu.VMEM((tt + CONV_HALO, D), F32), pltpu.VMEM((tt, D), F32)],
        compiler_params=_params(("parallel", "arbitrary"), 40),
        name="prompt_conv",
    )(u, u, w_dw, b_dw, g_cln, b_cln)


def _sconv_body(uv_ref, ug_ref, st_ref, w_ref, bdw_ref, g_ref, b_ref, u_ref, z_ref):
    u = uv_ref[...] * _sigmoid(ug_ref[...])
    acc = bdw_ref[...] + u * w_ref[HIST:HIST + 1, :]
    for j in range(HIST):
        acc = acc + st_ref[j] * w_ref[j:j + 1, :]
    u_ref[...] = u
    z_ref[...] = _ln_swish(acc, g_ref[...], b_ref[...])


def _sconv_call(u_val, u_gate, state_t, w_dw, b_dw, g_cln, b_cln):
    b = u_val.shape[0]
    return pl.pallas_call(
        _sconv_body,
        out_shape=[jax.ShapeDtypeStruct((b, D), F32), jax.ShapeDtypeStruct((b, D), F32)],
        compiler_params=pltpu.CompilerParams(vmem_limit_bytes=40 * MIB),
        name="sample_conv",
    )(u_val, u_gate, state_t, w_dw, b_dw, g_cln, b_cln)


def _router_body(xp_ref, xs_ref, g_ref, scp_ref, shp_ref, scs_ref, shs_ref, wr_ref, br_ref,
                 h2_ref, rid_ref, rwt_ref, *, n_prompt_tiles, n_sample):
    tm = xp_ref.shape[0]
    is_s = pl.program_id(0) == n_prompt_tiles
    x = jnp.where(is_s, xs_ref[...], xp_ref[...])
    sc = jnp.where(is_s, scs_ref[0], scp_ref[0])
    sh = jnp.where(is_s, shs_ref[0], shp_ref[0])
    n_valid = jnp.where(is_s, n_sample, tm)
    h2 = _rms(x, g_ref[...]) * (1.0 + sc) + sh
    h2_ref[...] = h2
    logits = _dot3(h2, wr_ref[...]) + br_ref[...]
    lane = lax.broadcasted_iota(I32, (tm, LANES), 1).astype(F32)
    far = float(LANES)
    gl = jnp.where(lane < N_GROUPS, logits, NEG)
    gmax = jnp.max(gl, axis=1, keepdims=True)
    grp = jnp.min(jnp.where(gl == gmax, lane, far), axis=1, keepdims=True)
    p_grp = 1.0 / jnp.sum(jnp.where(lane < N_GROUPS, jnp.exp(gl - gmax), 0.0), axis=1, keepdims=True)
    lo = N_GROUPS + grp * EPG
    el = jnp.where((lane >= lo) & (lane < lo + EPG), logits, NEG)
    t1 = jnp.max(el, axis=1, keepdims=True)
    i1 = jnp.min(jnp.where(el == t1, lane, far), axis=1, keepdims=True)
    el2 = jnp.where(lane == i1, NEG, el)
    t2 = jnp.max(el2, axis=1, keepdims=True)
    i2 = jnp.min(jnp.where(el2 == t2, lane, far), axis=1, keepdims=True)
    e = jnp.exp(t2 - t1)
    w1 = p_grp / (1.0 + e)
    w2 = p_grp * e / (1.0 + e)
    valid = lax.broadcasted_iota(I32, (tm, LANES), 0) < n_valid
    ids = jnp.where(lane == 0.0, i1 - N_GROUPS, jnp.where(lane == 1.0, i2 - N_GROUPS, -1.0))
    rid_ref[...] = jnp.where(valid, ids, -1.0).astype(I32)
    rwt_ref[...] = jnp.where(lane == 0.0, w1, jnp.where(lane == 1.0, w2, 0.0))


def _router_call(x_p, x_s, g, sc_p, sh_p, sc_s, sh_s, wr, br, *, seq, n_sample):
    tm = TOK_TILE
    npt = x_p.shape[0] // tm
    n_total = x_p.shape[0] + tm
    prow = lambda i: jnp.minimum(i, npt - 1)
    pmod = pl.BlockSpec((1, 1, D), lambda i: (prow(i) * tm // seq, 0, 0))
    smod = pl.BlockSpec((1, tm, D), lambda i: (0, 0, 0))
    return pl.pallas_call(
        functools.partial(_router_body, n_prompt_tiles=npt, n_sample=n_sample),
        grid=(npt + 1,),
        in_specs=[pl.BlockSpec((tm, D), lambda i: (prow(i), 0)), pl.BlockSpec((tm, D), lambda i: (0, 0)),
                  pl.BlockSpec((1, D), lambda i: (0, 0)), pmod, pmod, smod, smod,
                  pl.BlockSpec((D, LANES), lambda i: (0, 0)), pl.BlockSpec((1, LANES), lambda i: (0, 0))],
        out_specs=[pl.BlockSpec((tm, D), lambda i: (i, 0)), pl.BlockSpec((tm, LANES), lambda i: (i, 0)),
                   pl.BlockSpec((tm, LANES), lambda i: (i, 0))],
        out_shape=[jax.ShapeDtypeStruct((n_total, D), F32), jax.ShapeDtypeStruct((n_total, LANES), I32),
                   jax.ShapeDtypeStruct((n_total, LANES), F32)],
        compiler_params=_params(("arbitrary",), 40),
        name="router",
    )(x_p, x_s, g, sc_p, sh_p, sc_s, sh_s, wr, br)


def _rank_body(rid_ref, rank_ref, cnt_ref, carry):
    i = pl.program_id(0)
    tt = rid_ref.shape[0]

    @pl.when(i == 0)
    def _():
        carry[...] = jnp.zeros((1, LANES), F32)

    ids = rid_ref[...]
    lane = lax.broadcasted_iota(I32, (tt, LANES), 1)
    e1 = ids[:, 0:1]
    e2 = ids[:, 1:2]
    hit = jnp.where((lane == e1) | (lane == e2), 1.0, 0.0)
    row = lax.broadcasted_iota(I32, (tt, tt), 0)
    col = lax.broadcasted_iota(I32, (tt, tt), 1)
    before = jnp.where(col < row, 1.0, 0.0).astype(BF16)
    cum = _dot(before, hit.astype(BF16)) + carry[...]
    r1 = jnp.sum(jnp.where(lane == e1, cum, 0.0), axis=1, keepdims=True)
    r2 = jnp.sum(jnp.where(lane == e2, cum, 0.0), axis=1, keepdims=True)
    rank_ref[...] = jnp.where(lane == 0, r1, jnp.where(lane == 1, r2, 0.0)).astype(I32)
    total = carry[...] + jnp.sum(hit, axis=0, keepdims=True)
    carry[...] = total
    cnt_ref[...] = jnp.broadcast_to(total, cnt_ref.shape)


def _rank_call(rid):
    n = rid.shape[0]
    return pl.pallas_call(
        _rank_body,
        grid=(n // TOK_TILE,),
        in_specs=[pl.BlockSpec((TOK_TILE, LANES), lambda i: (i, 0))],
        out_specs=[pl.BlockSpec((TOK_TILE, LANES), lambda i: (i, 0)), pl.BlockSpec((8, LANES), lambda i: (0, 0))],
        out_shape=[jax.ShapeDtypeStruct((n, LANES), I32), jax.ShapeDtypeStruct((8, LANES), F32)],
        scratch_shapes=[pltpu.VMEM((1, LANES), F32)],
        compiler_params=_params(("arbitrary",), 32),
        name="moe_rank",
    )(rid)


def _row_copy(src_hbm, row, dst, slot, r, sem):
    return pltpu.make_async_copy(src_hbm.at[pl.ds(row, 1)], dst.at[slot, pl.ds(r, 1)], sem.at[slot])


def _gather_start(idx_ref, src_hbm, dst, slot, sem, n_rows):
    def body(r, _):
        _row_copy(src_hbm, idx_ref[0, 0, r], dst, slot, r, sem).start()
        return 0
    lax.fori_loop(0, n_rows, body, 0, unroll=8)


def _gather_wait(src_hbm, dst, slot, sem, n_rows):
    def body(r, _):
        _row_copy(src_hbm, 0, dst, slot, r, sem).wait()
        return 0
    lax.fori_loop(0, n_rows, body, 0, unroll=8)


def _expert_body(be_ref, nb_ref, idx_cur, idx_nxt, h_hbm, wg_ref, wu_ref, wd_ref, y_ref, xs, sem):
    i = pl.program_id(0)
    nb = nb_ref[0]
    slot = i % 2
    rows = xs.shape[1]

    @pl.when(i == 0)
    def _():
        _gather_start(idx_cur, h_hbm, xs, 0, sem, rows)

    @pl.when(i + 1 < nb)
    def _():
        _gather_start(idx_nxt, h_hbm, xs, 1 - slot, sem, rows)

    @pl.when(i < nb)
    def _():
        _gather_wait(h_hbm, xs, slot, sem, rows)
        x = xs[slot].astype(BF16)
        g = _dot(x, wg_ref[0])
        u = _dot(x, wu_ref[0])
        y_ref[...] = _dot((g * _sigmoid(g) * u).astype(BF16), wd_ref[0])

    @pl.when(i >= nb)
    def _():
        y_ref[...] = jnp.zeros(y_ref.shape, F32)


def _expert_call(block_expert, n_used, slot_token, h2, wg, wu, wd):
    n_blocks = slot_token.shape[0]
    idx_spec = lambda f: pl.BlockSpec((1, 1, MOE_ROWS), f, memory_space=pltpu.SMEM)
    grid_spec = pltpu.PrefetchScalarGridSpec(
        num_scalar_prefetch=2,
        grid=(n_blocks,),
        in_specs=[idx_spec(lambda i, be, nb: (i, 0, 0)),
                  idx_spec(lambda i, be, nb: (jnp.minimum(i + 1, n_blocks - 1), 0, 0)),
                  pl.BlockSpec(memory_space=pl.ANY),
                  pl.BlockSpec((1, D, D_EXPERT), lambda i, be, nb: (be[i], 0, 0)),
                  pl.BlockSpec((1, D, D_EXPERT), lambda i, be, nb: (be[i], 0, 0)),
                  pl.BlockSpec((1, D_EXPERT, D), lambda i, be, nb: (be[i], 0, 0))],
        out_specs=pl.BlockSpec((MOE_ROWS, D), lambda i, be, nb: (i, 0)),
        scratch_shapes=[pltpu.VMEM((2, MOE_ROWS, D), F32), pltpu.SemaphoreType.DMA((2,))],
    )
    return pl.pallas_call(
        _expert_body,
        grid_spec=grid_spec,
        out_shape=jax.ShapeDtypeStruct((n_blocks * MOE_ROWS, D), F32),
        compiler_params=_params(("arbitrary",), 56),
        name="moe_experts",
    )(block_expert, n_used, slot_token, slot_token, h2, wg, wu, wd)


def _combine_body(d_cur, d_nxt, ys_hbm, x_ref, g2_ref, wt_ref, gf_ref, o_ref, buf, sem, *, n_tiles):
    i = pl.program_id(0)
    slot = i % 2
    tt = x_ref.shape[0]

    @pl.when(i == 0)
    def _():
        _gather_start(d_cur, ys_hbm, buf, 0, sem, 2 * tt)

    @pl.when(i + 1 < n_tiles)
    def _():
        _gather_start(d_nxt, ys_hbm, buf, 1 - slot, sem, 2 * tt)

    _gather_wait(ys_hbm, buf, slot, sem, 2 * tt)
    wt = wt_ref[...]
    ffn = wt[:, 0:1] * buf[slot, 0:tt, :] + wt[:, 1:2] * buf[slot, tt:2 * tt, :]
    x = x_ref[...] + g2_ref[0] * ffn
    o_ref[...] = _rms(x, gf_ref[...])


def _combine_call(dest_tiles, ys, x, g2, rwt, g_final, *, rows_per_mod, tile_off):
    tt = COMB_TILE
    m = x.shape[0]
    n_tiles = m // tt
    r = g2.shape[1]
    idx_spec = lambda f: pl.BlockSpec((1, 1, 2 * tt), f, memory_space=pltpu.SMEM)
    return pl.pallas_call(
        functools.partial(_combine_body, n_tiles=n_tiles),
        grid=(n_tiles,),
        in_specs=[idx_spec(lambda i: (i + tile_off, 0, 0)),
                  idx_spec(lambda i: (jnp.minimum(i + 1, n_tiles - 1) + tile_off, 0, 0)),
                  pl.BlockSpec(memory_space=pl.ANY),
                  pl.BlockSpec((tt, D), lambda i: (i, 0)),
                  pl.BlockSpec((1, r, D), lambda i: (i * tt // rows_per_mod, 0, 0)),
                  pl.BlockSpec((tt, LANES), lambda i: (i + tile_off, 0)),
                  pl.BlockSpec((1, D), lambda i: (0, 0))],
        out_specs=pl.BlockSpec((tt, D), lambda i: (i, 0)),
        out_shape=jax.ShapeDtypeStruct((m, D), F32),
        scratch_shapes=[pltpu.VMEM((2, 2 * tt, D), F32), pltpu.SemaphoreType.DMA((2,))],
        compiler_params=_params(("arbitrary",), 40),
        name="moe_combine",
    )(dest_tiles, dest_tiles, ys, x, g2, rwt, g_final)


def kernel(x_prompt, x_sample, cache_k, cache_v, state_conv, page_table, c_prompt, c_sample, w_ada, b_ada, g_mix, w_in, lambda_q1, lambda_k1, lambda_q2, lambda_k2, g_head, w_attn_proj, w_dw, b_dw, g_conv_ln, b_conv_ln, w_conv_proj, w_out, g_ffn, w_group_router, b_group_router, w_expert_router, b_expert_router, w_gate_e, w_up_e, w_down_e, g_final):
    assert w_in.shape[0] == 1, "single-layer step"
    bp, seq, _ = x_prompt.shape
    bs = x_sample.shape[0]
    assert x_sample.shape[1] == 1
    tp = bp * seq
    n_pool, page = cache_k.shape[1], cache_k.shape[2]

    row = lambda a: a.reshape(1, -1)
    lam = (jnp.exp(jnp.sum(lambda_q1[0] * lambda_k1[0])) - jnp.exp(jnp.sum(lambda_q2[0] * lambda_k2[0]))
           + LAM_INIT).reshape(1).astype(F32)
    xp = x_prompt.reshape(tp, D)
    xs = x_sample.reshape(bs, D)

    c_all = jnp.concatenate([c_prompt, c_sample], axis=0)
    (mod,) = _mm_call([c_all], [(w_ada[0], 0)], [(0, 0)],
                      [(row(b_ada[0]), (1, 512), lambda i, j: (0, j))],
                      lambda accs, ex: [accs[0] + ex[0]], [F32], name="adaln_mod", tm=c_all.shape[0], tn=512, n_col=N_MOD * D // 512,
                      three_pass=True, prologue=lambda lhs, ex: [lhs[0] * _sigmoid(lhs[0])])
    mod_p = [mod[:bp, k * D:(k + 1) * D].reshape(bp, 1, D) for k in range(N_MOD)]
    mod_s = [mod[bp:, k * D:(k + 1) * D] for k in range(N_MOD)]

    full = lambda i, j: (0, 0)
    (proj_s,) = _mm_call([xs], [(w_in[0], 0)], [(0, 0)],
                         [(row(g_mix[0]), (1, D), full), (mod_s[1], (bs, D), full), (mod_s[0], (bs, D), full)],
                         lambda accs, ex: [accs[0]], [F32], name="sample_inproj", tm=bs, tn=512, n_col=N_IN // 512,
                         three_pass=True,
                         prologue=lambda lhs, ex: [_rms(lhs[0], ex[0]) * (1.0 + ex[1]) + ex[2]])
    seg = lambda k: proj_s[:, k * D:(k + 1) * D]
    q_s, k_s, v_s, uval_s, ugate_s, agate_s, cgate_s = (seg(k) for k in range(7))
    o_s = _decode_call(page_table, lam, q_s.reshape(bs, 1, D), k_s.reshape(bs, 1, D), v_s.reshape(bs, 1, D),
                       row(g_head[0]), cache_k.reshape(n_pool, page, D), cache_v.reshape(n_pool, page, D), n_pg=8)
    ident = lambda accs, ex: [accs[0]]
    (yattn_s,) = _mm_call([o_s.reshape(bs, D)], [(w_attn_proj[0], 0)], [(0, 0)], [], ident, [F32],
                          name="sample_attnproj", tm=bs, tn=512, n_col=D // 512, three_pass=True)
    w_dw_pad = jnp.pad(w_dw[0], ((0, CONV_HALO - CONV_W), (0, 0)))
    u_s, z_s = _sconv_call(uval_s, ugate_s, jnp.swapaxes(state_conv[0], 0, 1), w_dw_pad, row(b_dw[0]),
                           row(g_conv_ln[0]), row(b_conv_ln[0]))
    (yconv_s,) = _mm_call([z_s], [(w_conv_proj[0], 0)], [(0, 0)], [], ident, [F32],
                          name="sample_convproj", tm=bs, tn=512, n_col=D // 512, three_pass=True)
    tile_j = lambda i, j: (0, j)
    (x1_s,) = _mm_call([yattn_s, yconv_s], [(w_out[0], 0)], [(0, 0)],
                       [(agate_s, (bs, D), full), (cgate_s, (bs, D), full), (xs, (bs, 512), tile_j),
                        (mod_s[2], (bs, 512), tile_j)],
                       lambda accs, ex: [ex[2] + ex[3] * accs[0]], [F32], name="sample_outproj", tm=bs, tn=512, n_col=D // 512,
                       three_pass=True,
                       prologue=lambda lhs, ex: [_sigmoid(ex[0]) * lhs[0] + _sigmoid(ex[1]) * lhs[1]])

    w_in_b = w_in[0].astype(BF16)
    h_p = _normmod_call(xp, row(g_mix[0]), mod_p[1], mod_p[0], tm=512, rows_per_mod=seq, out_dtype=BF16)
    mm = functools.partial(_mm_call, [h_p])
    col = D // 1024
    (q_p,) = mm([(w_in_b, 0)], [(0, 0)], [], ident, [BF16], name="prompt_q", tm=1024, tn=1024, n_col=col)
    (k_p,) = mm([(w_in_b, col)], [(0, 0)], [], ident, [F32], name="prompt_k", tm=1024, tn=1024, n_col=col)
    (v_p,) = mm([(w_in_b, 2 * col)], [(0, 0)], [], ident, [F32], name="prompt_v", tm=1024, tn=1024, n_col=col)
    col5 = D // 512
    (u_p,) = mm([(w_in_b, 3 * col5), (w_in_b, 4 * col5)], [(0, 0), (0, 1)], [],
                lambda accs, ex: [accs[0] * _sigmoid(accs[1])], [F32], name="prompt_glu", tm=1024, tn=512, n_col=col5)
    (gates_p,) = mm([(w_in_b, 5 * col)], [(0, 0)], [], lambda accs, ex: [_sigmoid(accs[0])], [BF16],
                    name="prompt_gates", tm=1024, tn=1024, n_col=2 * col)
    o_p = _attn_call(lam, q_p, k_p, v_p, row(g_head[0]), batch=bp, seq=seq, tq=512)
    z_p = _conv_call(u_p.reshape(bp, seq, D), w_dw_pad, row(b_dw[0]), row(g_conv_ln[0]), row(b_conv_ln[0]),
                     tt=128).reshape(tp, D)
    (merged_p,) = _mm_call([o_p, z_p], [(w_attn_proj[0].astype(BF16), 0), (w_conv_proj[0].astype(BF16), 0)],
                           [(0, 0), (1, 1)],
                           [(gates_p, (1024, 512), lambda i, j: (i, j)),
                            (gates_p, (1024, 512), lambda i, j: (i, j + col5))],
                           lambda accs, ex: [ex[0].astype(F32) * accs[0] + ex[1].astype(F32) * accs[1]], [BF16],
                           name="prompt_merge", tm=1024, tn=512, n_col=col5)
    (x1_p,) = _mm_call([merged_p], [(w_out[0].astype(BF16), 0)], [(0, 0)],
                       [(xp, (1024, 1024), lambda i, j: (i, j)),
                        (mod_p[2], (1, 1, 1024), lambda i, j: (i * 1024 // seq, 0, j))],
                       lambda accs, ex: [ex[0] + ex[1][0] * accs[0]], [F32], name="prompt_outproj", tm=1024, tn=1024,
                       n_col=col)

    n_joint = tp + TOK_TILE
    wr = jnp.pad(jnp.concatenate([w_group_router[0], w_expert_router[0]], axis=1),
                 ((0, 0), (0, LANES - N_GROUPS - N_EXPERTS)))
    br = jnp.pad(jnp.concatenate([b_group_router[0], b_expert_router[0]]), (0, LANES - N_GROUPS - N_EXPERTS))
    pad_rows = lambda a, n: jnp.pad(a, ((0, n - a.shape[0]), (0, 0)))
    h2, rid, rwt = _router_call(x1_p, pad_rows(x1_s, TOK_TILE), row(g_ffn[0]), mod_p[4], mod_p[3],
                                pad_rows(mod_s[4], TOK_TILE)[None], pad_rows(mod_s[3], TOK_TILE)[None], wr, row(br),
                                seq=seq, n_sample=bs)
    rank, cnt = _rank_call(rid)

    n_pairs = 2 * (tp + bs)
    n_blocks = -(-n_pairs // MOE_ROWS) + N_EXPERTS
    counts = cnt[0, :N_EXPERTS].astype(I32)
    padded = (counts + MOE_ROWS - 1) // MOE_ROWS * MOE_ROWS
    pad_end = jnp.cumsum(padded)
    pad_start = pad_end - padded
    n_used = (pad_end[-1] // MOE_ROWS).astype(I32)
    expert = rid[:, :2]
    live = expert >= 0
    dest = jnp.where(live, pad_start[jnp.clip(expert, 0, N_EXPERTS - 1)] + rank[:, :2], 0)
    token = jnp.broadcast_to(jnp.arange(n_joint, dtype=I32)[:, None], dest.shape)
    slot_token = jnp.zeros((n_blocks * MOE_ROWS,), I32).at[jnp.where(live, dest, n_blocks * MOE_ROWS)].set(
        token, mode="drop")
    blk = jnp.arange(n_blocks, dtype=I32)
    block_expert = jnp.minimum(jnp.searchsorted(pad_end, jnp.minimum(blk, n_used - 1) * MOE_ROWS, side="right"),
                               N_EXPERTS - 1).astype(I32)
    ys = _expert_call(block_expert, n_used.reshape(1), slot_token.reshape(n_blocks, 1, MOE_ROWS), h2,
                      w_gate_e[0].astype(BF16), w_up_e[0].astype(BF16), w_down_e[0].astype(BF16))

    dest_tiles = dest.reshape(n_joint // COMB_TILE, COMB_TILE, 2).transpose(0, 2, 1).reshape(
        n_joint // COMB_TILE, 1, 2 * COMB_TILE)
    y_p = _combine_call(dest_tiles, ys, x1_p, mod_p[5], rwt, row(g_final), rows_per_mod=seq, tile_off=0)
    y_s = _combine_call(dest_tiles, ys, pad_rows(x1_s, COMB_TILE), pad_rows(mod_s[5], COMB_TILE)[None], rwt,
                        row(g_final), rows_per_mod=COMB_TILE, tile_off=tp // COMB_TILE)[:bs]

    conv_p = u_p.reshape(bp, seq, D)[:, seq - HIST:, :]
    conv_s = jnp.concatenate([state_conv[0][:, 1:, :], u_s[:, None, :]], axis=1)
    return (y_p.reshape(bp, seq, D), y_s.reshape(bs, 1, D),
            k_p.reshape(1, bp, seq, H, 2, HD), v_p.reshape(1, bp, seq, H, VD), conv_p[None],
            k_s.reshape(1, bs, 1, H, 2, HD), v_s.reshape(1, bs, 1, H, VD), conv_s[None])
```

```python
import functools
import math

import jax
import jax.numpy as jnp
from jax import lax
from jax.experimental import pallas as pl
from jax.experimental.pallas import tpu as pltpu

F32 = jnp.float32
BF16 = jnp.bfloat16
I32 = jnp.int32

D = 2048
H = 8
HD = 128
VD = 2 * HD
N_MAPS = 2 * H
CONV_W = 31
HIST = CONV_W - 1
N_GROUPS = 4
EPG = 8
N_EXPERTS = N_GROUPS * EPG
D_EXPERT = D // 2
N_MOD = 6
N_IN = 7 * D
EPS = 1e-6
HEAD_EPS = 1e-5
LAM_INIT = 0.8 - 0.6 * math.exp(0.0)
NEG = -1e30
LANES = 128
SUBLANES = 8
MIB = 1024 * 1024

MOE_ROWS = 256
TOK_TILE = 256
COMB_TILE = 128


def _params(sem, vmem_mb):
    return pltpu.CompilerParams(dimension_semantics=sem, vmem_limit_bytes=vmem_mb * MIB)


def _dot(a, b):
    return jnp.dot(a.astype(BF16), b.astype(BF16), preferred_element_type=F32)


def _sigmoid(x):
    return 1.0 / (1.0 + jnp.exp(-x))


def _rms(x, g):
    return x * lax.rsqrt(jnp.mean(x * x, axis=-1, keepdims=True) + EPS) * g


def _mm_body(*refs, n_lhs, n_w, n_ex, pairs, prologue, epilogue):
    lhs = [r[...] for r in refs[:n_lhs]]
    w_refs = refs[n_lhs:n_lhs + n_w]
    ex = [r[...] for r in refs[n_lhs + n_w:n_lhs + n_w + n_ex]]
    outs = refs[n_lhs + n_w + n_ex:]
    if prologue is not None:
        lhs = prologue(lhs, ex)
    accs = []
    for li, wi in pairs:
        accs.append(_dot(lhs[li], w_refs[wi][...]))
    for r, o in zip(outs, epilogue(accs, ex)):
        r[...] = o.astype(r.dtype)


def _mm_call(lhs, ws, pairs, extras, epilogue, out_dtypes, *, name, tm, tn, n_col, prologue=None, vmem_mb=48):
    m = lhs[0].shape[0]
    in_specs = [pl.BlockSpec((tm, a.shape[1]), lambda i, j: (i, 0)) for a in lhs]
    in_specs += [pl.BlockSpec((w.shape[0], tn), functools.partial(lambda i, j, off: (0, j + off), off=off))
                 for w, off in ws]
    in_specs += [pl.BlockSpec(blk, imap) for _, blk, imap in extras]
    body = functools.partial(_mm_body, n_lhs=len(lhs), n_w=len(ws), n_ex=len(extras), pairs=pairs,
                             prologue=prologue, epilogue=epilogue)
    outs = pl.pallas_call(
        body,
        grid=(m // tm, n_col),
        in_specs=in_specs,
        out_specs=[pl.BlockSpec((tm, tn), lambda i, j: (i, j)) for _ in out_dtypes],
        out_shape=[jax.ShapeDtypeStruct((m, n_col * tn), dt) for dt in out_dtypes],
        compiler_params=_params(("parallel", "arbitrary"), vmem_mb),
        name=name,
    )(*lhs, *[w for w, _ in ws], *[a for a, _, _ in extras])
    return outs


def _normmod_body(x_ref, g_ref, sc_ref, sh_ref, o_ref):
    y = _rms(x_ref[...], g_ref[...])
    o_ref[...] = (y * (1.0 + sc_ref[0]) + sh_ref[0]).astype(o_ref.dtype)


def _normmod_call(x, g, sc, sh, *, tm, rows_per_mod, out_dtype):
    m = x.shape[0]
    r = sc.shape[1]
    mod_spec = pl.BlockSpec((1, r, D), lambda i: (i * tm // rows_per_mod, 0, 0))
    return pl.pallas_call(
        _normmod_body,
        grid=(m // tm,),
        in_specs=[pl.BlockSpec((tm, D), lambda i: (i, 0)), pl.BlockSpec((1, D), lambda i: (0, 0)), mod_spec, mod_spec],
        out_specs=pl.BlockSpec((tm, D), lambda i: (i, 0)),
        out_shape=jax.ShapeDtypeStruct((m, D), out_dtype),
        compiler_params=_params(("parallel",), 40),
        name="normmod",
    )(x, g, sc, sh)


def _attn_body(lam_ref, q_ref, k_ref, v_ref, gh_ref, o_ref, kb, vb, *, tq, scale):
    qi = pl.program_id(2)

    @pl.when(qi == 0)
    def _():
        kb[...] = k_ref[...].astype(BF16)
        vb[...] = v_ref[...].astype(BF16)

    q = q_ref[...]
    lam = lam_ref[0]
    nt = (((1,), (1,)), ((), ()))

    def step(kc, carry, masked):
        off = pl.multiple_of(kc * tq, tq)
        ks = kb[pl.ds(off, tq), :]
        vs = vb[pl.ds(off, tq), :]
        new = []
        for mp in range(2):
            m, l, a = carry[mp]
            s = lax.dot_general(q[:, mp * HD:(mp + 1) * HD], ks[:, mp * HD:(mp + 1) * HD], nt,
                                preferred_element_type=F32) * scale
            if masked:
                row = lax.broadcasted_iota(I32, (tq, tq), 0)
                col = lax.broadcasted_iota(I32, (tq, tq), 1)
                s = jnp.where(col <= row, s, NEG)
            mn = jnp.maximum(m, jnp.max(s, axis=1, keepdims=True))
            alpha = jnp.exp(m - mn)
            p = jnp.exp(s - mn)
            l = alpha * l + jnp.sum(p, axis=1, keepdims=True)
            a = alpha * a + _dot(p.astype(BF16), vs)
            new.append((mn, l, a))
        return tuple(new)

    init = tuple((jnp.full((tq, 1), NEG, F32), jnp.zeros((tq, 1), F32), jnp.zeros((tq, VD), F32)) for _ in range(2))
    carry = lax.fori_loop(0, qi, lambda kc, c: step(kc, c, False), init)
    (_, l0, a0), (_, l1, a1) = step(qi, carry, True)
    o = a0 / l0 - lam * (a1 / l1)
    o = o * lax.rsqrt(jnp.mean(o * o, axis=1, keepdims=True) + HEAD_EPS) * gh_ref[...] * (1.0 - LAM_INIT)
    o_ref[...] = o.astype(o_ref.dtype)


def _attn_call(lam, q, k, v, g_head, *, batch, seq, tq):
    nq = seq // tq
    body = functools.partial(_attn_body, tq=tq, scale=HD ** -0.5)
    return pl.pallas_call(
        body,
        grid=(batch, H, nq),
        in_specs=[
            pl.BlockSpec(memory_space=pltpu.SMEM),
            pl.BlockSpec((tq, VD), lambda b, h, i: (b * nq + i, h)),
            pl.BlockSpec((seq, VD), lambda b, h, i: (b, h)),
            pl.BlockSpec((seq, VD), lambda b, h, i: (b, h)),
            pl.BlockSpec((1, VD), lambda b, h, i: (0, 0)),
        ],
        out_specs=pl.BlockSpec((tq, VD), lambda b, h, i: (b * nq + i, h)),
        out_shape=jax.ShapeDtypeStruct((batch * seq, D), BF16),
        scratch_shapes=[pltpu.VMEM((seq, VD), BF16), pltpu.VMEM((seq, VD), BF16)],
        compiler_params=_params(("parallel", "parallel", "arbitrary"), 40),
        name="prompt_attn",
    )(lam, q, k, v, g_head)


def _page_columns(ref):
    keys = ref.shape[0]
    flat = ref.reshape(keys * 8, LANES)
    return [flat[pl.ds(r, keys, stride=8), :] for r in range(8)]


def _k_page(lo_ref, hi_ref):
    return jnp.concatenate(_page_columns(lo_ref) + _page_columns(hi_ref), axis=1).astype(BF16)


def _v_page(left_ref, right_ref):
    left, right = _page_columns(left_ref), _page_columns(right_ref)
    return jnp.concatenate([part for h in range(H) for part in (left[h], right[h])], axis=1).astype(BF16)


def _decode_body(pt_ref, lam_ref, q_ref, kn_ref, vn_ref, gh_ref, *rest, n_pg, scale):
    k_refs = rest[:2 * n_pg]
    v_refs = rest[2 * n_pg:4 * n_pg]
    o_ref = rest[4 * n_pg]
    qb, m_s, l_s, acc_s = rest[4 * n_pg + 1:]
    j = pl.program_id(1)
    nt = (((1,), (1,)), ((), ()))

    @pl.when(j == 0)
    def _():
        row = lax.broadcasted_iota(I32, (N_MAPS, D), 0)
        col = lax.broadcasted_iota(I32, (N_MAPS, D), 1)
        qb[...] = jnp.where(col // HD == row, jnp.broadcast_to(q_ref[0], (N_MAPS, D)), 0.0).astype(BF16)
        m_s[...] = jnp.full((N_MAPS, 1), NEG, F32)
        l_s[...] = jnp.zeros((N_MAPS, 1), F32)
        acc_s[...] = jnp.zeros((N_MAPS, D), F32)

    qv = qb[...]
    s = jnp.concatenate(
        [lax.dot_general(qv, _k_page(k_refs[2 * g], k_refs[2 * g + 1]), nt, preferred_element_type=F32)
         for g in range(n_pg)], axis=1) * scale
    m_old = m_s[...]
    m_new = jnp.maximum(m_old, jnp.max(s, axis=1, keepdims=True))
    alpha = jnp.exp(m_old - m_new)
    p = jnp.exp(s - m_new)
    l_s[...] = alpha * l_s[...] + jnp.sum(p, axis=1, keepdims=True)
    pb = p.astype(BF16)
    pv = _dot(pb[:, 0:LANES], _v_page(v_refs[0], v_refs[1]))
    for g in range(1, n_pg):
        pv = pv + _dot(pb[:, g * LANES:(g + 1) * LANES], _v_page(v_refs[2 * g], v_refs[2 * g + 1]))
    acc_s[...] = alpha * acc_s[...] + pv
    m_s[...] = m_new

    @pl.when(j == pl.num_programs(1) - 1)
    def _():
        qf = qb[...].astype(F32)
        kn = kn_ref[0].astype(BF16).astype(F32)
        vn = vn_ref[0].astype(BF16).astype(F32)
        s_new = jnp.sum(qf * kn, axis=1, keepdims=True) * scale
        m_fin = jnp.maximum(m_s[...], s_new)
        a = jnp.exp(m_s[...] - m_fin)
        p_new = jnp.exp(s_new - m_fin)
        l_fin = l_s[...] * a + p_new
        accn = (acc_s[...] * a + p_new.astype(BF16).astype(F32) * vn) / l_fin
        lam = lam_ref[0]
        for h in range(H):
            cs = slice(h * VD, (h + 1) * VD)
            o = accn[2 * h:2 * h + 1, cs] - lam * accn[2 * h + 1:2 * h + 2, cs]
            o = o * lax.rsqrt(jnp.mean(o * o, axis=1, keepdims=True) + HEAD_EPS) * gh_ref[...] * (1.0 - LAM_INIT)
            o_ref[0, :, cs] = o


def _decode_call(page_table, lam, q, k_new, v_new, g_head, cache_k, cache_v, *, n_pool, n_pg):
    b, n_pages = page_table.shape
    page = cache_k.shape[0] // n_pool
    row_spec = pl.BlockSpec((1, 1, D), lambda bi, j, pt: (bi, 0, 0))
    page_specs = [pl.BlockSpec((page, 8, LANES),
                               functools.partial(lambda bi, j, pt, g, half: (pt[bi, j * n_pg + g], half, 0), g=g, half=half))
                  for g in range(n_pg) for half in range(2)]
    body = functools.partial(_decode_body, n_pg=n_pg, scale=HD ** -0.5)
    grid_spec = pltpu.PrefetchScalarGridSpec(
        num_scalar_prefetch=1,
        grid=(b, n_pages // n_pg),
        in_specs=[pl.BlockSpec(memory_space=pltpu.SMEM), row_spec, row_spec, row_spec,
                  pl.BlockSpec((1, VD), lambda bi, j, pt: (0, 0))] + page_specs + page_specs,
        out_specs=pl.BlockSpec((1, 1, D), lambda bi, j, pt: (bi, 0, 0)),
        scratch_shapes=[pltpu.VMEM((N_MAPS, D), BF16), pltpu.VMEM((N_MAPS, 1), F32), pltpu.VMEM((N_MAPS, 1), F32),
                        pltpu.VMEM((N_MAPS, D), F32)],
    )
    return pl.pallas_call(
        body,
        grid_spec=grid_spec,
        out_shape=jax.ShapeDtypeStruct((b, 1, D), F32),
        compiler_params=_params(("parallel", "arbitrary"), 56),
        name="decode_attn",
    )(page_table, lam, q, k_new, v_new, g_head, *([cache_k] * (2 * n_pg)), *([cache_v] * (2 * n_pg)))


CONV_HALO = 32
CONV_RC = 32
CONV_CB = 512


def _ln_swish(acc, g, b):
    xc = acc - jnp.mean(acc, axis=1, keepdims=True)
    y = xc * lax.rsqrt(jnp.mean(xc * xc, axis=1, keepdims=True) + EPS) * g + b
    return y * _sigmoid(y)


def _conv_body(cur_ref, prev_ref, w_ref, bdw_ref, g_ref, b_ref, z_ref, full, ph, uc, *, tt):
    t = pl.program_id(1)
    full[0:CONV_HALO, :] = jnp.where(t > 0, prev_ref[0], 0.0).astype(BF16).astype(F32)
    full[CONV_HALO:, :] = cur_ref[0].astype(BF16).astype(F32)
    rows = ph.shape[1]
    for s in range(1, SUBLANES):
        for cb in range(D // CONV_CB):
            cs = slice(cb * CONV_CB, (cb + 1) * CONV_CB)
            ph[s - 1, :, cs] = full[s:s + rows, cs]
    first = CONV_HALO - HIST

    def chunk(c, _):
        r0 = pl.multiple_of(c * CONV_RC, CONV_RC)
        for cb in range(D // CONV_CB):
            cs = slice(cb * CONV_CB, (cb + 1) * CONV_CB)
            accs = [bdw_ref[:, cs] for _ in range(CONV_RC // SUBLANES)]
            for j in range(CONV_W):
                a, s = divmod(first + j, SUBLANES)
                src = full if s == 0 else ph.at[s - 1]
                wj = w_ref[j, :, cs]
                for rg in range(CONV_RC // SUBLANES):
                    accs[rg] = accs[rg] + src[pl.ds(r0 + SUBLANES * (a + rg), SUBLANES), cs] * wj
            for rg in range(CONV_RC // SUBLANES):
                uc[rg * SUBLANES:(rg + 1) * SUBLANES, cs] = accs[rg]
        z_ref[0, pl.ds(r0, CONV_RC), :] = _ln_swish(uc[...], g_ref[...], b_ref[...]).astype(z_ref.dtype)
        return 0

    lax.fori_loop(0, tt // CONV_RC, chunk, 0)


def _conv_call(u, w_dw, b_dw, g_cln, b_cln, *, tt):
    b, s, _ = u.shape
    ratio = tt // CONV_HALO
    vec = pl.BlockSpec((1, D), lambda bi, t: (0, 0))
    return pl.pallas_call(
        functools.partial(_conv_body, tt=tt),
        grid=(b, s // tt),
        in_specs=[pl.BlockSpec((1, tt, D), lambda bi, t: (bi, t, 0)),
                  pl.BlockSpec((1, CONV_HALO, D), lambda bi, t: (bi, jnp.maximum(t * ratio - 1, 0), 0)),
                  pl.BlockSpec((CONV_HALO, SUBLANES, D), lambda bi, t: (0, 0, 0)),
                  pl.BlockSpec((SUBLANES, D), lambda bi, t: (0, 0)), vec, vec],
        out_specs=pl.BlockSpec((1, tt, D), lambda bi, t: (bi, t, 0)),
        out_shape=jax.ShapeDtypeStruct((b, s, D), BF16),
        scratch_shapes=[pltpu.VMEM((tt + CONV_HALO, D), F32),
                        pltpu.VMEM((SUBLANES - 1, tt + CONV_HALO - SUBLANES, D), F32),
                        pltpu.VMEM((CONV_RC, D), F32)],
        compiler_params=_params(("parallel", "arbitrary"), 40),
        name="prompt_conv",
    )(u, u, w_dw, b_dw, g_cln, b_cln)


def _sconv_body(uv_ref, ug_ref, st_ref, w_ref, bdw_ref, g_ref, b_ref, u_ref, z_ref):
    u = uv_ref[...] * _sigmoid(ug_ref[...])
    rnd = lambda a: a.astype(BF16).astype(F32)
    acc = bdw_ref[...] + rnd(u) * w_ref[HIST:HIST + 1, :]
    for j in range(HIST):
        acc = acc + rnd(st_ref[j]) * w_ref[j:j + 1, :]
    u_ref[...] = u
    z_ref[...] = _ln_swish(acc, g_ref[...], b_ref[...])


def _sconv_call(u_val, u_gate, state_t, w_dw, b_dw, g_cln, b_cln):
    b = u_val.shape[0]
    return pl.pallas_call(
        _sconv_body,
        out_shape=[jax.ShapeDtypeStruct((b, D), F32), jax.ShapeDtypeStruct((b, D), F32)],
        compiler_params=pltpu.CompilerParams(vmem_limit_bytes=40 * MIB),
        name="sample_conv",
    )(u_val, u_gate, state_t, w_dw, b_dw, g_cln, b_cln)


def _router_body(xp_ref, xs_ref, g_ref, scp_ref, shp_ref, scs_ref, shs_ref, wr_ref, br_ref,
                 h2_ref, rid_ref, rwt_ref, *, n_prompt_tiles, n_sample):
    tm = xp_ref.shape[0]
    is_s = pl.program_id(0) == n_prompt_tiles
    x = jnp.where(is_s, xs_ref[...], xp_ref[...])
    sc = jnp.where(is_s, scs_ref[0], scp_ref[0])
    sh = jnp.where(is_s, shs_ref[0], shp_ref[0])
    n_valid = jnp.where(is_s, n_sample, tm)
    h2 = _rms(x, g_ref[...]) * (1.0 + sc) + sh
    h2_ref[...] = h2
    logits = _dot(h2, wr_ref[...]) + br_ref[...]
    lane = lax.broadcasted_iota(I32, (tm, LANES), 1).astype(F32)
    far = float(LANES)
    gl = jnp.where(lane < N_GROUPS, logits, NEG)
    gmax = jnp.max(gl, axis=1, keepdims=True)
    grp = jnp.min(jnp.where(gl == gmax, lane, far), axis=1, keepdims=True)
    p_grp = 1.0 / jnp.sum(jnp.where(lane < N_GROUPS, jnp.exp(gl - gmax), 0.0), axis=1, keepdims=True)
    lo = N_GROUPS + grp * EPG
    el = jnp.where((lane >= lo) & (lane < lo + EPG), logits, NEG)
    t1 = jnp.max(el, axis=1, keepdims=True)
    i1 = jnp.min(jnp.where(el == t1, lane, far), axis=1, keepdims=True)
    el2 = jnp.where(lane == i1, NEG, el)
    t2 = jnp.max(el2, axis=1, keepdims=True)
    i2 = jnp.min(jnp.where(el2 == t2, lane, far), axis=1, keepdims=True)
    e = jnp.exp(t2 - t1)
    w1 = p_grp / (1.0 + e)
    w2 = p_grp * e / (1.0 + e)
    valid = lax.broadcasted_iota(I32, (tm, LANES), 0) < n_valid
    ids = jnp.where(lane == 0.0, i1 - N_GROUPS, jnp.where(lane == 1.0, i2 - N_GROUPS, -1.0))
    rid_ref[...] = jnp.where(valid, ids, -1.0).astype(I32)
    rwt_ref[...] = jnp.where(lane == 0.0, w1, jnp.where(lane == 1.0, w2, 0.0))


def _router_call(x_p, x_s, g, sc_p, sh_p, sc_s, sh_s, wr, br, *, seq, n_sample):
    tm = TOK_TILE
    npt = x_p.shape[0] // tm
    n_total = x_p.shape[0] + tm
    prow = lambda i: jnp.minimum(i, npt - 1)
    pmod = pl.BlockSpec((1, 1, D), lambda i: (prow(i) * tm // seq, 0, 0))
    smod = pl.BlockSpec((1, tm, D), lambda i: (0, 0, 0))
    return pl.pallas_call(
        functools.partial(_router_body, n_prompt_tiles=npt, n_sample=n_sample),
        grid=(npt + 1,),
        in_specs=[pl.BlockSpec((tm, D), lambda i: (prow(i), 0)), pl.BlockSpec((tm, D), lambda i: (0, 0)),
                  pl.BlockSpec((1, D), lambda i: (0, 0)), pmod, pmod, smod, smod,
                  pl.BlockSpec((D, LANES), lambda i: (0, 0)), pl.BlockSpec((1, LANES), lambda i: (0, 0))],
        out_specs=[pl.BlockSpec((tm, D), lambda i: (i, 0)), pl.BlockSpec((tm, LANES), lambda i: (i, 0)),
                   pl.BlockSpec((tm, LANES), lambda i: (i, 0))],
        out_shape=[jax.ShapeDtypeStruct((n_total, D), F32), jax.ShapeDtypeStruct((n_total, LANES), I32),
                   jax.ShapeDtypeStruct((n_total, LANES), F32)],
        compiler_params=_params(("arbitrary",), 40),
        name="router",
    )(x_p, x_s, g, sc_p, sh_p, sc_s, sh_s, wr, br)


def _rank_body(rid_ref, rank_ref, cnt_ref, carry):
    i = pl.program_id(0)
    tt = rid_ref.shape[0]

    @pl.when(i == 0)
    def _():
        carry[...] = jnp.zeros((1, LANES), F32)

    ids = rid_ref[...]
    lane = lax.broadcasted_iota(I32, (tt, LANES), 1)
    e1 = ids[:, 0:1]
    e2 = ids[:, 1:2]
    hit = jnp.where((lane == e1) | (lane == e2), 1.0, 0.0)
    row = lax.broadcasted_iota(I32, (tt, tt), 0)
    col = lax.broadcasted_iota(I32, (tt, tt), 1)
    before = jnp.where(col < row, 1.0, 0.0).astype(BF16)
    cum = _dot(before, hit.astype(BF16)) + carry[...]
    r1 = jnp.sum(jnp.where(lane == e1, cum, 0.0), axis=1, keepdims=True)
    r2 = jnp.sum(jnp.where(lane == e2, cum, 0.0), axis=1, keepdims=True)
    rank_ref[...] = jnp.where(lane == 0, r1, jnp.where(lane == 1, r2, 0.0)).astype(I32)
    total = carry[...] + jnp.sum(hit, axis=0, keepdims=True)
    carry[...] = total
    cnt_ref[...] = jnp.broadcast_to(total, cnt_ref.shape)


def _rank_call(rid):
    n = rid.shape[0]
    return pl.pallas_call(
        _rank_body,
        grid=(n // TOK_TILE,),
        in_specs=[pl.BlockSpec((TOK_TILE, LANES), lambda i: (i, 0))],
        out_specs=[pl.BlockSpec((TOK_TILE, LANES), lambda i: (i, 0)), pl.BlockSpec((8, LANES), lambda i: (0, 0))],
        out_shape=[jax.ShapeDtypeStruct((n, LANES), I32), jax.ShapeDtypeStruct((8, LANES), F32)],
        scratch_shapes=[pltpu.VMEM((1, LANES), F32)],
        compiler_params=_params(("arbitrary",), 32),
        name="moe_rank",
    )(rid)


def _row_copy(src_hbm, row, dst, slot, r, sem):
    return pltpu.make_async_copy(src_hbm.at[pl.ds(row, 1)], dst.at[slot, pl.ds(r, 1)], sem.at[slot])


def _gather_start(idx_ref, src_hbm, dst, slot, sem, n_rows):
    def body(r, _):
        _row_copy(src_hbm, idx_ref[0, 0, r], dst, slot, r, sem).start()
        return 0
    lax.fori_loop(0, n_rows, body, 0, unroll=8)


def _gather_wait(src_hbm, dst, slot, sem, n_rows):
    def body(r, _):
        _row_copy(src_hbm, 0, dst, slot, r, sem).wait()
        return 0
    lax.fori_loop(0, n_rows, body, 0, unroll=8)


def _expert_body(be_ref, nb_ref, idx_cur, idx_nxt, h_hbm, wg_ref, wu_ref, wd_ref, y_ref, xs, sem):
    i = pl.program_id(0)
    nb = nb_ref[0]
    slot = i % 2
    rows = xs.shape[1]

    @pl.when(i == 0)
    def _():
        _gather_start(idx_cur, h_hbm, xs, 0, sem, rows)

    @pl.when(i + 1 < nb)
    def _():
        _gather_start(idx_nxt, h_hbm, xs, 1 - slot, sem, rows)

    @pl.when(i < nb)
    def _():
        _gather_wait(h_hbm, xs, slot, sem, rows)
        x = xs[slot].astype(BF16)
        g = _dot(x, wg_ref[0])
        u = _dot(x, wu_ref[0])
        y_ref[...] = _dot((g * _sigmoid(g) * u).astype(BF16), wd_ref[0])

    @pl.when(i >= nb)
    def _():
        y_ref[...] = jnp.zeros(y_ref.shape, F32)


def _expert_call(block_expert, n_used, slot_token, h2, wg, wu, wd):
    n_blocks = slot_token.shape[0]
    idx_spec = lambda f: pl.BlockSpec((1, 1, MOE_ROWS), f, memory_space=pltpu.SMEM)
    grid_spec = pltpu.PrefetchScalarGridSpec(
        num_scalar_prefetch=2,
        grid=(n_blocks,),
        in_specs=[idx_spec(lambda i, be, nb: (i, 0, 0)),
                  idx_spec(lambda i, be, nb: (jnp.minimum(i + 1, n_blocks - 1), 0, 0)),
                  pl.BlockSpec(memory_space=pl.ANY),
                  pl.BlockSpec((1, D, D_EXPERT), lambda i, be, nb: (be[i], 0, 0)),
                  pl.BlockSpec((1, D, D_EXPERT), lambda i, be, nb: (be[i], 0, 0)),
                  pl.BlockSpec((1, D_EXPERT, D), lambda i, be, nb: (be[i], 0, 0))],
        out_specs=pl.BlockSpec((MOE_ROWS, D), lambda i, be, nb: (i, 0)),
        scratch_shapes=[pltpu.VMEM((2, MOE_ROWS, D), F32), pltpu.SemaphoreType.DMA((2,))],
    )
    return pl.pallas_call(
        _expert_body,
        grid_spec=grid_spec,
        out_shape=jax.ShapeDtypeStruct((n_blocks * MOE_ROWS, D), F32),
        compiler_params=_params(("arbitrary",), 56),
        name="moe_experts",
    )(block_expert, n_used, slot_token, slot_token, h2, wg, wu, wd)


def _combine_body(d_cur, d_nxt, ys_hbm, x_ref, g2_ref, wt_ref, gf_ref, o_ref, buf, sem, *, n_tiles):
    i = pl.program_id(0)
    slot = i % 2
    tt = x_ref.shape[0]

    @pl.when(i == 0)
    def _():
        _gather_start(d_cur, ys_hbm, buf, 0, sem, 2 * tt)

    @pl.when(i + 1 < n_tiles)
    def _():
        _gather_start(d_nxt, ys_hbm, buf, 1 - slot, sem, 2 * tt)

    _gather_wait(ys_hbm, buf, slot, sem, 2 * tt)
    wt = wt_ref[...]
    ffn = wt[:, 0:1] * buf[slot, 0:tt, :] + wt[:, 1:2] * buf[slot, tt:2 * tt, :]
    x = x_ref[...] + g2_ref[0] * ffn
    o_ref[...] = _rms(x, gf_ref[...])


def _combine_call(dest_tiles, ys, x, g2, rwt, g_final, *, rows_per_mod, tile_off):
    tt = COMB_TILE
    m = x.shape[0]
    n_tiles = m // tt
    r = g2.shape[1]
    idx_spec = lambda f: pl.BlockSpec((1, 1, 2 * tt), f, memory_space=pltpu.SMEM)
    return pl.pallas_call(
        functools.partial(_combine_body, n_tiles=n_tiles),
        grid=(n_tiles,),
        in_specs=[idx_spec(lambda i: (i + tile_off, 0, 0)),
                  idx_spec(lambda i: (jnp.minimum(i + 1, n_tiles - 1) + tile_off, 0, 0)),
                  pl.BlockSpec(memory_space=pl.ANY),
                  pl.BlockSpec((tt, D), lambda i: (i, 0)),
                  pl.BlockSpec((1, r, D), lambda i: (i * tt // rows_per_mod, 0, 0)),
                  pl.BlockSpec((tt, LANES), lambda i: (i + tile_off, 0)),
                  pl.BlockSpec((1, D), lambda i: (0, 0))],
        out_specs=pl.BlockSpec((tt, D), lambda i: (i, 0)),
        out_shape=jax.ShapeDtypeStruct((m, D), F32),
        scratch_shapes=[pltpu.VMEM((2, 2 * tt, D), F32), pltpu.SemaphoreType.DMA((2,))],
        compiler_params=_params(("arbitrary",), 40),
        name="moe_combine",
    )(dest_tiles, dest_tiles, ys, x, g2, rwt, g_final)


def kernel(x_prompt, x_sample, cache_k, cache_v, state_conv, page_table, c_prompt, c_sample, w_ada, b_ada, g_mix, w_in, lambda_q1, lambda_k1, lambda_q2, lambda_k2, g_head, w_attn_proj, w_dw, b_dw, g_conv_ln, b_conv_ln, w_conv_proj, w_out, g_ffn, w_group_router, b_group_router, w_expert_router, b_expert_router, w_gate_e, w_up_e, w_down_e, g_final):
    assert w_in.shape[0] == 1, "single-layer step"
    bp, seq, _ = x_prompt.shape
    bs = x_sample.shape[0]
    assert x_sample.shape[1] == 1
    tp = bp * seq
    n_pool, page = cache_k.shape[1], cache_k.shape[2]

    row = lambda a: a.reshape(1, -1)
    lam = (jnp.exp(jnp.sum(lambda_q1[0] * lambda_k1[0])) - jnp.exp(jnp.sum(lambda_q2[0] * lambda_k2[0]))
           + LAM_INIT).reshape(1).astype(F32)
    xp = x_prompt.reshape(tp, D)
    xs = x_sample.reshape(bs, D)

    c_all = jnp.concatenate([c_prompt, c_sample], axis=0)
    (mod,) = _mm_call([c_all], [(w_ada[0], 0)], [(0, 0)],
                      [(row(b_ada[0]), (1, 512), lambda i, j: (0, j))],
                      lambda accs, ex: [accs[0] + ex[0]], [F32], name="adaln_mod", tm=c_all.shape[0], tn=512, n_col=N_MOD * D // 512,
                      prologue=lambda lhs, ex: [lhs[0] * _sigmoid(lhs[0])])
    mod_p = [mod[:bp, k * D:(k + 1) * D].reshape(bp, 1, D) for k in range(N_MOD)]
    mod_s = [mod[bp:, k * D:(k + 1) * D] for k in range(N_MOD)]

    full = lambda i, j: (0, 0)
    (proj_s,) = _mm_call([xs], [(w_in[0], 0)], [(0, 0)],
                         [(row(g_mix[0]), (1, D), full), (mod_s[1], (bs, D), full), (mod_s[0], (bs, D), full)],
                         lambda accs, ex: [accs[0]], [F32], name="sample_inproj", tm=bs, tn=512, n_col=N_IN // 512,
                         prologue=lambda lhs, ex: [_rms(lhs[0], ex[0]) * (1.0 + ex[1]) + ex[2]])
    seg = lambda k: proj_s[:, k * D:(k + 1) * D]
    q_s, k_s, v_s, uval_s, ugate_s, agate_s, cgate_s = (seg(k) for k in range(7))
    o_s = _decode_call(page_table, lam, q_s.reshape(bs, 1, D), k_s.reshape(bs, 1, D), v_s.reshape(bs, 1, D),
                       row(g_head[0]), cache_k.reshape(n_pool * page, N_MAPS, HD),
                       cache_v.reshape(n_pool * page, H, 2, LANES).transpose(0, 2, 1, 3).reshape(n_pool * page, 2 * H, LANES),
                       n_pool=n_pool, n_pg=4)
    ident = lambda accs, ex: [accs[0]]
    (yattn_s,) = _mm_call([o_s.reshape(bs, D)], [(w_attn_proj[0], 0)], [(0, 0)], [], ident, [F32],
                          name="sample_attnproj", tm=bs, tn=512, n_col=D // 512)
    w_dw_pad = jnp.pad(w_dw[0], ((0, CONV_HALO - CONV_W), (0, 0)))
    u_s, z_s = _sconv_call(uval_s, ugate_s, jnp.swapaxes(state_conv[0], 0, 1), w_dw_pad, row(b_dw[0]),
                           row(g_conv_ln[0]), row(b_conv_ln[0]))
    (yconv_s,) = _mm_call([z_s], [(w_conv_proj[0], 0)], [(0, 0)], [], ident, [F32],
                          name="sample_convproj", tm=bs, tn=512, n_col=D // 512)
    tile_j = lambda i, j: (0, j)
    (x1_s,) = _mm_call([yattn_s, yconv_s], [(w_out[0], 0)], [(0, 0)],
                       [(agate_s, (bs, D), full), (cgate_s, (bs, D), full), (xs, (bs, 512), tile_j),
                        (mod_s[2], (bs, 512), tile_j)],
                       lambda accs, ex: [ex[2] + ex[3] * accs[0]], [F32], name="sample_outproj", tm=bs, tn=512, n_col=D // 512,
                       prologue=lambda lhs, ex: [_sigmoid(ex[0]) * lhs[0] + _sigmoid(ex[1]) * lhs[1]])

    w_in_b = w_in[0].astype(BF16)
    h_p = _normmod_call(xp, row(g_mix[0]), mod_p[1], mod_p[0], tm=512, rows_per_mod=seq, out_dtype=BF16)
    mm = functools.partial(_mm_call, [h_p])
    col = D // 1024
    (q_p,) = mm([(w_in_b, 0)], [(0, 0)], [], ident, [BF16], name="prompt_q", tm=1024, tn=1024, n_col=col)
    (k_p,) = mm([(w_in_b, col)], [(0, 0)], [], ident, [F32], name="prompt_k", tm=1024, tn=1024, n_col=col)
    (v_p,) = mm([(w_in_b, 2 * col)], [(0, 0)], [], ident, [F32], name="prompt_v", tm=1024, tn=1024, n_col=col)
    col5 = D // 512
    (u_p,) = mm([(w_in_b, 3 * col5), (w_in_b, 4 * col5)], [(0, 0), (0, 1)], [],
                lambda accs, ex: [accs[0] * _sigmoid(accs[1])], [F32], name="prompt_glu", tm=1024, tn=512, n_col=col5)
    (gates_p,) = mm([(w_in_b, 5 * col)], [(0, 0)], [], lambda accs, ex: [_sigmoid(accs[0])], [BF16],
                    name="prompt_gates", tm=1024, tn=1024, n_col=2 * col)
    o_p = _attn_call(lam, q_p, k_p, v_p, row(g_head[0]), batch=bp, seq=seq, tq=512)
    z_p = _conv_call(u_p.reshape(bp, seq, D), jnp.broadcast_to(w_dw_pad[:, None, :], (CONV_HALO, SUBLANES, D)),
                     jnp.broadcast_to(row(b_dw[0]), (SUBLANES, D)), row(g_conv_ln[0]), row(b_conv_ln[0]),
                     tt=256).reshape(tp, D)
    (merged_p,) = _mm_call([o_p, z_p], [(w_attn_proj[0].astype(BF16), 0), (w_conv_proj[0].astype(BF16), 0)],
                           [(0, 0), (1, 1)],
                           [(gates_p, (1024, 512), lambda i, j: (i, j)),
                            (gates_p, (1024, 512), lambda i, j: (i, j + col5))],
                           lambda accs, ex: [ex[0].astype(F32) * accs[0] + ex[1].astype(F32) * accs[1]], [BF16],
                           name="prompt_merge", tm=1024, tn=512, n_col=col5)
    (x1_p,) = _mm_call([merged_p], [(w_out[0].astype(BF16), 0)], [(0, 0)],
                       [(xp, (1024, 1024), lambda i, j: (i, j)),
                        (mod_p[2], (1, 1, 1024), lambda i, j: (i * 1024 // seq, 0, j))],
                       lambda accs, ex: [ex[0] + ex[1][0] * accs[0]], [F32], name="prompt_outproj", tm=1024, tn=1024,
                       n_col=col)

    n_joint = tp + TOK_TILE
    wr = jnp.pad(jnp.concatenate([w_group_router[0], w_expert_router[0]], axis=1),
                 ((0, 0), (0, LANES - N_GROUPS - N_EXPERTS)))
    br = jnp.pad(jnp.concatenate([b_group_router[0], b_expert_router[0]]), (0, LANES - N_GROUPS - N_EXPERTS))
    pad_rows = lambda a, n: jnp.pad(a, ((0, n - a.shape[0]), (0, 0)))
    h2, rid, rwt = _router_call(x1_p, pad_rows(x1_s, TOK_TILE), row(g_ffn[0]), mod_p[4], mod_p[3],
                                pad_rows(mod_s[4], TOK_TILE)[None], pad_rows(mod_s[3], TOK_TILE)[None], wr, row(br),
                                seq=seq, n_sample=bs)
    rank, cnt = _rank_call(rid)

    n_pairs = 2 * (tp + bs)
    n_blocks = -(-n_pairs // MOE_ROWS) + N_EXPERTS
    counts = cnt[0, :N_EXPERTS].astype(I32)
    padded = (counts + MOE_ROWS - 1) // MOE_ROWS * MOE_ROWS
    pad_end = jnp.cumsum(padded)
    pad_start = pad_end - padded
    n_used = (pad_end[-1] // MOE_ROWS).astype(I32)
    expert = rid[:, :2]
    live = expert >= 0
    dest = jnp.where(live, pad_start[jnp.clip(expert, 0, N_EXPERTS - 1)] + rank[:, :2], 0)
    token = jnp.broadcast_to(jnp.arange(n_joint, dtype=I32)[:, None], dest.shape)
    slot_token = jnp.zeros((n_blocks * MOE_ROWS,), I32).at[jnp.where(live, dest, n_blocks * MOE_ROWS)].set(
        token, mode="drop")
    blk = jnp.arange(n_blocks, dtype=I32)
    block_expert = jnp.minimum(jnp.searchsorted(pad_end, jnp.minimum(blk, n_used - 1) * MOE_ROWS, side="right"),
                               N_EXPERTS - 1).astype(I32)
    ys = _expert_call(block_expert, n_used.reshape(1), slot_token.reshape(n_blocks, 1, MOE_ROWS), h2,
                      w_gate_e[0].astype(BF16), w_up_e[0].astype(BF16), w_down_e[0].astype(BF16))

    dest_tiles = dest.reshape(n_joint // COMB_TILE, COMB_TILE, 2).transpose(0, 2, 1).reshape(
        n_joint // COMB_TILE, 1, 2 * COMB_TILE)
    y_p = _combine_call(dest_tiles, ys, x1_p, mod_p[5], rwt, row(g_final), rows_per_mod=seq, tile_off=0)
    y_s = _combine_call(dest_tiles, ys, pad_rows(x1_s, COMB_TILE), pad_rows(mod_s[5], COMB_TILE)[None], rwt,
                        row(g_final), rows_per_mod=COMB_TILE, tile_off=tp // COMB_TILE)[:bs]

    conv_p = u_p.reshape(bp, seq, D)[:, seq - HIST:, :]
    conv_s = jnp.concatenate([state_conv[0][:, 1:, :], u_s[:, None, :]], axis=1)
    return (y_p.reshape(bp, seq, D), y_s.reshape(bs, 1, D),
            k_p.reshape(1, bp, seq, H, 2, HD), v_p.reshape(1, bp, seq, H, VD), conv_p[None],
            k_s.reshape(1, bs, 1, H, 2, HD), v_s.reshape(1, bs, 1, H, VD), conv_s[None])
```

```python
import functools
import math

import jax
import jax.numpy as jnp
from jax import lax
from jax.experimental import pallas as pl
from jax.experimental.pallas import tpu as pltpu

F32 = jnp.float32
BF16 = jnp.bfloat16
I32 = jnp.int32

D = 2048
H = 8
HD = 128
VD = 2 * HD
N_MAPS = 2 * H
CONV_W = 31
HIST = CONV_W - 1
N_GROUPS = 4
EPG = 8
N_EXPERTS = N_GROUPS * EPG
D_EXPERT = D // 2
N_MOD = 6
N_IN = 7 * D
EPS = 1e-6
HEAD_EPS = 1e-5
LAM_INIT = 0.8 - 0.6 * math.exp(0.0)
NEG = -1e30
LANES = 128
SUBLANES = 8
MIB = 1024 * 1024

MOE_ROWS = 256
TOK_TILE = 256
COMB_TILE = 128


def _params(sem, vmem_mb):
    return pltpu.CompilerParams(dimension_semantics=sem, vmem_limit_bytes=vmem_mb * MIB)


def _dot(a, b):
    return jnp.dot(a.astype(BF16), b.astype(BF16), preferred_element_type=F32)


def _sigmoid(x):
    return 1.0 / (1.0 + jnp.exp(-x))


def _rms(x, g):
    return x * lax.rsqrt(jnp.mean(x * x, axis=-1, keepdims=True) + EPS) * g


def _mm_body(*refs, n_lhs, n_w, n_ex, pairs, prologue, epilogue):
    lhs = [r[...] for r in refs[:n_lhs]]
    w_refs = refs[n_lhs:n_lhs + n_w]
    ex = [r[...] for r in refs[n_lhs + n_w:n_lhs + n_w + n_ex]]
    outs = refs[n_lhs + n_w + n_ex:]
    if prologue is not None:
        lhs = prologue(lhs, ex)
    accs = []
    for li, wi in pairs:
        accs.append(_dot(lhs[li], w_refs[wi][...]))
    for r, o in zip(outs, epilogue(accs, ex)):
        r[...] = o.astype(r.dtype)


def _mm_call(lhs, ws, pairs, extras, epilogue, out_dtypes, *, name, tm, tn, n_col, prologue=None, vmem_mb=48):
    m = lhs[0].shape[0]
    in_specs = [pl.BlockSpec((tm, a.shape[1]), lambda i, j: (i, 0)) for a in lhs]
    in_specs += [pl.BlockSpec((w.shape[0], tn), functools.partial(lambda i, j, off: (0, j + off), off=off))
                 for w, off in ws]
    in_specs += [pl.BlockSpec(blk, imap) for _, blk, imap in extras]
    body = functools.partial(_mm_body, n_lhs=len(lhs), n_w=len(ws), n_ex=len(extras), pairs=pairs,
                             prologue=prologue, epilogue=epilogue)
    outs = pl.pallas_call(
        body,
        grid=(m // tm, n_col),
        in_specs=in_specs,
        out_specs=[pl.BlockSpec((tm, tn), lambda i, j: (i, j)) for _ in out_dtypes],
        out_shape=[jax.ShapeDtypeStruct((m, n_col * tn), dt) for dt in out_dtypes],
        compiler_params=_params(("parallel", "arbitrary"), vmem_mb),
        name=name,
    )(*lhs, *[w for w, _ in ws], *[a for a, _, _ in extras])
    return outs


def _normmod_body(x_ref, g_ref, sc_ref, sh_ref, o_ref):
    y = _rms(x_ref[...], g_ref[...])
    o_ref[...] = (y * (1.0 + sc_ref[0]) + sh_ref[0]).astype(o_ref.dtype)


def _normmod_call(x, g, sc, sh, *, tm, rows_per_mod, out_dtype):
    m = x.shape[0]
    r = sc.shape[1]
    mod_spec = pl.BlockSpec((1, r, D), lambda i: (i * tm // rows_per_mod, 0, 0))
    return pl.pallas_call(
        _normmod_body,
        grid=(m // tm,),
        in_specs=[pl.BlockSpec((tm, D), lambda i: (i, 0)), pl.BlockSpec((1, D), lambda i: (0, 0)), mod_spec, mod_spec],
        out_specs=pl.BlockSpec((tm, D), lambda i: (i, 0)),
        out_shape=jax.ShapeDtypeStruct((m, D), out_dtype),
        compiler_params=_params(("parallel",), 40),
        name="normmod",
    )(x, g, sc, sh)


def _strided_rows(ref, first, n, stride):
    flat = ref.reshape(ref.shape[0] * ref.shape[1], LANES)
    return flat[pl.ds(first, n, stride=stride), :]


def _store_strided_rows(ref, first, stride, val):
    flat = ref.reshape(ref.shape[0] * ref.shape[1], LANES)
    flat[pl.ds(first, val.shape[0], stride=stride), :] = val


def _kproj_body(h_ref, w_ref, o_ref):
    acc = _dot(h_ref[...], w_ref[...])
    for c in range(o_ref.shape[1]):
        _store_strided_rows(o_ref, c, o_ref.shape[1], acc[:, c * HD:(c + 1) * HD])


def _kproj_call(h, w, col_off, *, tm):
    m = h.shape[0]
    half = N_MAPS // 2
    return pl.pallas_call(
        _kproj_body,
        grid=(m // tm, 2),
        in_specs=[pl.BlockSpec((tm, D), lambda i, j: (i, 0)),
                  pl.BlockSpec((D, half * HD), lambda i, j: (0, j + col_off))],
        out_specs=pl.BlockSpec((tm, half, HD), lambda i, j: (i, j, 0)),
        out_shape=jax.ShapeDtypeStruct((m, N_MAPS, HD), F32),
        compiler_params=_params(("parallel", "arbitrary"), 48),
        name="prompt_k",
    )(h, w)


def _attn_body(lam_ref, q_ref, k_ref, v_ref, gh_ref, o_ref, kb, vb, *, tq, scale):
    qi = pl.program_id(2)
    head = pl.program_id(1)

    @pl.when(qi == 0)
    def _():
        for mp in range(2):
            kb[:, mp * HD:(mp + 1) * HD] = _strided_rows(k_ref, 2 * head + mp, k_ref.shape[0], N_MAPS).astype(BF16)
        vb[...] = v_ref[...].astype(BF16)

    q = q_ref[...]
    lam = lam_ref[0]
    nt = (((1,), (1,)), ((), ()))

    def step(kc, carry, masked):
        off = pl.multiple_of(kc * tq, tq)
        ks = kb[pl.ds(off, tq), :]
        vs = vb[pl.ds(off, tq), :]
        new = []
        for mp in range(2):
            m, l, a = carry[mp]
            s = lax.dot_general(q[:, mp * HD:(mp + 1) * HD], ks[:, mp * HD:(mp + 1) * HD], nt,
                                preferred_element_type=F32) * scale
            if masked:
                row = lax.broadcasted_iota(I32, (tq, tq), 0)
                col = lax.broadcasted_iota(I32, (tq, tq), 1)
                s = jnp.where(col <= row, s, NEG)
            mn = jnp.maximum(m, jnp.max(s, axis=1, keepdims=True))
            alpha = jnp.exp(m - mn)
            p = jnp.exp(s - mn)
            l = alpha * l + jnp.sum(p, axis=1, keepdims=True)
            a = alpha * a + _dot(p.astype(BF16), vs)
            new.append((mn, l, a))
        return tuple(new)

    init = tuple((jnp.full((tq, 1), NEG, F32), jnp.zeros((tq, 1), F32), jnp.zeros((tq, VD), F32)) for _ in range(2))
    carry = lax.fori_loop(0, qi, lambda kc, c: step(kc, c, False), init)
    (_, l0, a0), (_, l1, a1) = step(qi, carry, True)
    o = a0 / l0 - lam * (a1 / l1)
    o = o * lax.rsqrt(jnp.mean(o * o, axis=1, keepdims=True) + HEAD_EPS) * gh_ref[...] * (1.0 - LAM_INIT)
    o_ref[...] = o.astype(o_ref.dtype)


def _attn_call(lam, q, k, v, g_head, *, batch, seq, tq):
    nq = seq // tq
    body = functools.partial(_attn_body, tq=tq, scale=HD ** -0.5)
    return pl.pallas_call(
        body,
        grid=(batch, H, nq),
        in_specs=[
            pl.BlockSpec(memory_space=pltpu.SMEM),
            pl.BlockSpec((tq, VD), lambda b, h, i: (b * nq + i, h)),
            pl.BlockSpec((seq, N_MAPS, HD), lambda b, h, i: (b, 0, 0)),
            pl.BlockSpec((seq, VD), lambda b, h, i: (b, h)),
            pl.BlockSpec((1, VD), lambda b, h, i: (0, 0)),
        ],
        out_specs=pl.BlockSpec((tq, VD), lambda b, h, i: (b * nq + i, h)),
        out_shape=jax.ShapeDtypeStruct((batch * seq, D), BF16),
        scratch_shapes=[pltpu.VMEM((seq, VD), BF16), pltpu.VMEM((seq, VD), BF16)],
        compiler_params=_params(("parallel", "arbitrary", "arbitrary"), 56),
        name="prompt_attn",
    )(lam, q, k, v, g_head)


def _page_columns(ref):
    keys = ref.shape[0]
    flat = ref.reshape(keys * 8, LANES)
    return [flat[pl.ds(r, keys, stride=8), :] for r in range(8)]


def _k_page(lo_ref, hi_ref):
    return jnp.concatenate(_page_columns(lo_ref) + _page_columns(hi_ref), axis=1).astype(BF16)


def _v_page(left_ref, right_ref):
    left, right = _page_columns(left_ref), _page_columns(right_ref)
    return jnp.concatenate([part for h in range(H) for part in (left[h], right[h])], axis=1).astype(BF16)


def _decode_body(pt_ref, lam_ref, q_ref, kn_ref, vn_ref, gh_ref, *rest, n_pg, scale):
    k_refs = rest[:2 * n_pg]
    v_refs = rest[2 * n_pg:4 * n_pg]
    o_ref = rest[4 * n_pg]
    qb, m_s, l_s, acc_s = rest[4 * n_pg + 1:]
    j = pl.program_id(1)
    nt = (((1,), (1,)), ((), ()))

    @pl.when(j == 0)
    def _():
        row = lax.broadcasted_iota(I32, (N_MAPS, D), 0)
        col = lax.broadcasted_iota(I32, (N_MAPS, D), 1)
        qb[...] = jnp.where(col // HD == row, jnp.broadcast_to(q_ref[0], (N_MAPS, D)), 0.0).astype(BF16)
        m_s[...] = jnp.full((N_MAPS, 1), NEG, F32)
        l_s[...] = jnp.zeros((N_MAPS, 1), F32)
        acc_s[...] = jnp.zeros((N_MAPS, D), F32)

    qv = qb[...]
    s = jnp.concatenate(
        [lax.dot_general(qv, _k_page(k_refs[2 * g], k_refs[2 * g + 1]), nt, preferred_element_type=F32)
         for g in range(n_pg)], axis=1) * scale
    m_old = m_s[...]
    m_new = jnp.maximum(m_old, jnp.max(s, axis=1, keepdims=True))
    alpha = jnp.exp(m_old - m_new)
    p = jnp.exp(s - m_new)
    l_s[...] = alpha * l_s[...] + jnp.sum(p, axis=1, keepdims=True)
    pb = p.astype(BF16)
    pv = _dot(pb[:, 0:LANES], _v_page(v_refs[0], v_refs[1]))
    for g in range(1, n_pg):
        pv = pv + _dot(pb[:, g * LANES:(g + 1) * LANES], _v_page(v_refs[2 * g], v_refs[2 * g + 1]))
    acc_s[...] = alpha * acc_s[...] + pv
    m_s[...] = m_new

    @pl.when(j == pl.num_programs(1) - 1)
    def _():
        qf = qb[...].astype(F32)
        kn = kn_ref[0].astype(BF16).astype(F32)
        vn = vn_ref[0].astype(BF16).astype(F32)
        s_new = jnp.sum(qf * kn, axis=1, keepdims=True) * scale
        m_fin = jnp.maximum(m_s[...], s_new)
        a = jnp.exp(m_s[...] - m_fin)
        p_new = jnp.exp(s_new - m_fin)
        l_fin = l_s[...] * a + p_new
        accn = (acc_s[...] * a + p_new.astype(BF16).astype(F32) * vn) / l_fin
        lam = lam_ref[0]
        for h in range(H):
            cs = slice(h * VD, (h + 1) * VD)
            o = accn[2 * h:2 * h + 1, cs] - lam * accn[2 * h + 1:2 * h + 2, cs]
            o = o * lax.rsqrt(jnp.mean(o * o, axis=1, keepdims=True) + HEAD_EPS) * gh_ref[...] * (1.0 - LAM_INIT)
            o_ref[0, :, cs] = o


def _decode_call(page_table, lam, q, k_new, v_new, g_head, cache_k, cache_v, *, n_pool, n_pg):
    b, n_pages = page_table.shape
    page = cache_k.shape[0] // n_pool
    row_spec = pl.BlockSpec((1, 1, D), lambda bi, j, pt: (bi, 0, 0))
    page_specs = [pl.BlockSpec((page, 8, LANES),
                               functools.partial(lambda bi, j, pt, g, half: (pt[bi, j * n_pg + g], half, 0), g=g, half=half))
                  for g in range(n_pg) for half in range(2)]
    body = functools.partial(_decode_body, n_pg=n_pg, scale=HD ** -0.5)
    grid_spec = pltpu.PrefetchScalarGridSpec(
        num_scalar_prefetch=1,
        grid=(b, n_pages // n_pg),
        in_specs=[pl.BlockSpec(memory_space=pltpu.SMEM), row_spec, row_spec, row_spec,
                  pl.BlockSpec((1, VD), lambda bi, j, pt: (0, 0))] + page_specs + page_specs,
        out_specs=pl.BlockSpec((1, 1, D), lambda bi, j, pt: (bi, 0, 0)),
        scratch_shapes=[pltpu.VMEM((N_MAPS, D), BF16), pltpu.VMEM((N_MAPS, 1), F32), pltpu.VMEM((N_MAPS, 1), F32),
                        pltpu.VMEM((N_MAPS, D), F32)],
    )
    return pl.pallas_call(
        body,
        grid_spec=grid_spec,
        out_shape=jax.ShapeDtypeStruct((b, 1, D), F32),
        compiler_params=_params(("parallel", "arbitrary"), 56),
        name="decode_attn",
    )(page_table, lam, q, k_new, v_new, g_head, *([cache_k] * (2 * n_pg)), *([cache_v] * (2 * n_pg)))


CONV_HALO = 32
CONV_RC = 32
CONV_CB = 512


def _ln_swish(acc, g, b):
    xc = acc - jnp.mean(acc, axis=1, keepdims=True)
    y = xc * lax.rsqrt(jnp.mean(xc * xc, axis=1, keepdims=True) + EPS) * g + b
    return y * _sigmoid(y)


def _conv_body(cur_ref, prev_ref, w_ref, bdw_ref, g_ref, b_ref, z_ref, full, ph, uc, *, tt):
    t = pl.program_id(1)
    full[0:CONV_HALO, :] = jnp.where(t > 0, prev_ref[0], 0.0).astype(BF16).astype(F32)
    full[CONV_HALO:, :] = cur_ref[0].astype(BF16).astype(F32)
    rows = ph.shape[1]
    for s in range(1, SUBLANES):
        for cb in range(D // CONV_CB):
            cs = slice(cb * CONV_CB, (cb + 1) * CONV_CB)
            ph[s - 1, :, cs] = full[s:s + rows, cs]
    first = CONV_HALO - HIST

    def chunk(c, _):
        r0 = pl.multiple_of(c * CONV_RC, CONV_RC)
        for cb in range(D // CONV_CB):
            cs = slice(cb * CONV_CB, (cb + 1) * CONV_CB)
            accs = [bdw_ref[:, cs] for _ in range(CONV_RC // SUBLANES)]
            for j in range(CONV_W):
                a, s = divmod(first + j, SUBLANES)
                src = full if s == 0 else ph.at[s - 1]
                wj = w_ref[j, :, cs]
                for rg in range(CONV_RC // SUBLANES):
                    accs[rg] = accs[rg] + src[pl.ds(r0 + SUBLANES * (a + rg), SUBLANES), cs] * wj
            for rg in range(CONV_RC // SUBLANES):
                uc[rg * SUBLANES:(rg + 1) * SUBLANES, cs] = accs[rg]
        z_ref[0, pl.ds(r0, CONV_RC), :] = _ln_swish(uc[...], g_ref[...], b_ref[...]).astype(z_ref.dtype)
        return 0

    lax.fori_loop(0, tt // CONV_RC, chunk, 0)


def _conv_call(u, w_dw, b_dw, g_cln, b_cln, *, tt):
    b, s, _ = u.shape
    ratio = tt // CONV_HALO
    vec = pl.BlockSpec((1, D), lambda bi, t: (0, 0))
    return pl.pallas_call(
        functools.partial(_conv_body, tt=tt),
        grid=(b, s // tt),
        in_specs=[pl.BlockSpec((1, tt, D), lambda bi, t: (bi, t, 0)),
                  pl.BlockSpec((1, CONV_HALO, D), lambda bi, t: (bi, jnp.maximum(t * ratio - 1, 0), 0)),
                  pl.BlockSpec((CONV_HALO, SUBLANES, D), lambda bi, t: (0, 0, 0)),
                  pl.BlockSpec((SUBLANES, D), lambda bi, t: (0, 0)), vec, vec],
        out_specs=pl.BlockSpec((1, tt, D), lambda bi, t: (bi, t, 0)),
        out_shape=jax.ShapeDtypeStruct((b, s, D), BF16),
        scratch_shapes=[pltpu.VMEM((tt + CONV_HALO, D), F32),
                        pltpu.VMEM((SUBLANES - 1, tt + CONV_HALO - SUBLANES, D), F32),
                        pltpu.VMEM((CONV_RC, D), F32)],
        compiler_params=_params(("parallel", "arbitrary"), 40),
        name="prompt_conv",
    )(u, u, w_dw, b_dw, g_cln, b_cln)


def _sconv_body(uv_ref, ug_ref, st_ref, w_ref, bdw_ref, g_ref, b_ref, u_ref, z_ref):
    u = uv_ref[...] * _sigmoid(ug_ref[...])
    rnd = lambda a: a.astype(BF16).astype(F32)
    acc = bdw_ref[...] + rnd(u) * w_ref[HIST:HIST + 1, :]
    for j in range(HIST):
        acc = acc + rnd(st_ref[j]) * w_ref[j:j + 1, :]
    u_ref[...] = u
    z_ref[...] = _ln_swish(acc, g_ref[...], b_ref[...])


def _sconv_call(u_val, u_gate, state_t, w_dw, b_dw, g_cln, b_cln):
    b = u_val.shape[0]
    return pl.pallas_call(
        _sconv_body,
        out_shape=[jax.ShapeDtypeStruct((b, D), F32), jax.ShapeDtypeStruct((b, D), F32)],
        compiler_params=pltpu.CompilerParams(vmem_limit_bytes=40 * MIB),
        name="sample_conv",
    )(u_val, u_gate, state_t, w_dw, b_dw, g_cln, b_cln)


def _router_body(xp_ref, xs_ref, g_ref, scp_ref, shp_ref, scs_ref, shs_ref, wr_ref, br_ref,
                 h2_ref, rid_ref, rwt_ref, *, n_prompt_tiles, n_sample):
    tm = xp_ref.shape[0]
    is_s = pl.program_id(0) == n_prompt_tiles
    x = jnp.where(is_s, xs_ref[...], xp_ref[...])
    sc = jnp.where(is_s, scs_ref[0], scp_ref[0])
    sh = jnp.where(is_s, shs_ref[0], shp_ref[0])
    n_valid = jnp.where(is_s, n_sample, tm)
    h2 = _rms(x, g_ref[...]) * (1.0 + sc) + sh
    h2_ref[...] = h2
    logits = _dot(h2, wr_ref[...]) + br_ref[...]
    lane = lax.broadcasted_iota(I32, (tm, LANES), 1).astype(F32)
    far = float(LANES)
    gl = jnp.where(lane < N_GROUPS, logits, NEG)
    gmax = jnp.max(gl, axis=1, keepdims=True)
    grp = jnp.min(jnp.where(gl == gmax, lane, far), axis=1, keepdims=True)
    p_grp = 1.0 / jnp.sum(jnp.where(lane < N_GROUPS, jnp.exp(gl - gmax), 0.0), axis=1, keepdims=True)
    lo = N_GROUPS + grp * EPG
    el = jnp.where((lane >= lo) & (lane < lo + EPG), logits, NEG)
    t1 = jnp.max(el, axis=1, keepdims=True)
    i1 = jnp.min(jnp.where(el == t1, lane, far), axis=1, keepdims=True)
    el2 = jnp.where(lane == i1, NEG, el)
    t2 = jnp.max(el2, axis=1, keepdims=True)
    i2 = jnp.min(jnp.where(el2 == t2, lane, far), axis=1, keepdims=True)
    e = jnp.exp(t2 - t1)
    w1 = p_grp / (1.0 + e)
    w2 = p_grp * e / (1.0 + e)
    valid = lax.broadcasted_iota(I32, (tm, LANES), 0) < n_valid
    ids = jnp.where(lane == 0.0, i1 - N_GROUPS, jnp.where(lane == 1.0, i2 - N_GROUPS, -1.0))
    rid_ref[...] = jnp.where(valid, ids, -1.0).astype(I32)
    rwt_ref[...] = jnp.where(lane == 0.0, w1, jnp.where(lane == 1.0, w2, 0.0))


def _router_call(x_p, x_s, g, sc_p, sh_p, sc_s, sh_s, wr, br, *, seq, n_sample):
    tm = TOK_TILE
    npt = x_p.shape[0] // tm
    n_total = x_p.shape[0] + tm
    prow = lambda i: jnp.minimum(i, npt - 1)
    pmod = pl.BlockSpec((1, 1, D), lambda i: (prow(i) * tm // seq, 0, 0))
    smod = pl.BlockSpec((1, tm, D), lambda i: (0, 0, 0))
    return pl.pallas_call(
        functools.partial(_router_body, n_prompt_tiles=npt, n_sample=n_sample),
        grid=(npt + 1,),
        in_specs=[pl.BlockSpec((tm, D), lambda i: (prow(i), 0)), pl.BlockSpec((tm, D), lambda i: (0, 0)),
                  pl.BlockSpec((1, D), lambda i: (0, 0)), pmod, pmod, smod, smod,
                  pl.BlockSpec((D, LANES), lambda i: (0, 0)), pl.BlockSpec((1, LANES), lambda i: (0, 0))],
        out_specs=[pl.BlockSpec((tm, D), lambda i: (i, 0)), pl.BlockSpec((tm, LANES), lambda i: (i, 0)),
                   pl.BlockSpec((tm, LANES), lambda i: (i, 0))],
        out_shape=[jax.ShapeDtypeStruct((n_total, D), F32), jax.ShapeDtypeStruct((n_total, LANES), I32),
                   jax.ShapeDtypeStruct((n_total, LANES), F32)],
        compiler_params=_params(("arbitrary",), 40),
        name="router",
    )(x_p, x_s, g, sc_p, sh_p, sc_s, sh_s, wr, br)


def _rank_body(rid_ref, rank_ref, cnt_ref, carry):
    i = pl.program_id(0)
    tt = rid_ref.shape[0]

    @pl.when(i == 0)
    def _():
        carry[...] = jnp.zeros((1, LANES), F32)

    ids = rid_ref[...]
    lane = lax.broadcasted_iota(I32, (tt, LANES), 1)
    e1 = ids[:, 0:1]
    e2 = ids[:, 1:2]
    hit = jnp.where((lane == e1) | (lane == e2), 1.0, 0.0)
    row = lax.broadcasted_iota(I32, (tt, tt), 0)
    col = lax.broadcasted_iota(I32, (tt, tt), 1)
    before = jnp.where(col < row, 1.0, 0.0).astype(BF16)
    cum = _dot(before, hit.astype(BF16)) + carry[...]
    r1 = jnp.sum(jnp.where(lane == e1, cum, 0.0), axis=1, keepdims=True)
    r2 = jnp.sum(jnp.where(lane == e2, cum, 0.0), axis=1, keepdims=True)
    rank_ref[...] = jnp.where(lane == 0, r1, jnp.where(lane == 1, r2, 0.0)).astype(I32)
    total = carry[...] + jnp.sum(hit, axis=0, keepdims=True)
    carry[...] = total
    cnt_ref[...] = jnp.broadcast_to(total, cnt_ref.shape)


def _rank_call(rid):
    n = rid.shape[0]
    return pl.pallas_call(
        _rank_body,
        grid=(n // TOK_TILE,),
        in_specs=[pl.BlockSpec((TOK_TILE, LANES), lambda i: (i, 0))],
        out_specs=[pl.BlockSpec((TOK_TILE, LANES), lambda i: (i, 0)), pl.BlockSpec((8, LANES), lambda i: (0, 0))],
        out_shape=[jax.ShapeDtypeStruct((n, LANES), I32), jax.ShapeDtypeStruct((8, LANES), F32)],
        scratch_shapes=[pltpu.VMEM((1, LANES), F32)],
        compiler_params=_params(("arbitrary",), 32),
        name="moe_rank",
    )(rid)


def _row_copy(src_hbm, row, dst, slot, r, sem):
    return pltpu.make_async_copy(src_hbm.at[pl.ds(row, 1)], dst.at[slot, pl.ds(r, 1)], sem.at[slot])


def _gather_start(idx_ref, src_hbm, dst, slot, sem, n_rows):
    def body(r, _):
        _row_copy(src_hbm, idx_ref[0, 0, r], dst, slot, r, sem).start()
        return 0
    lax.fori_loop(0, n_rows, body, 0, unroll=8)


def _gather_wait(src_hbm, dst, slot, sem, n_rows):
    def body(r, _):
        _row_copy(src_hbm, 0, dst, slot, r, sem).wait()
        return 0
    lax.fori_loop(0, n_rows, body, 0, unroll=8)


def _expert_body(be_ref, nb_ref, idx_cur, idx_nxt, h_hbm, wg_ref, wu_ref, wd_ref, y_ref, xs, sem):
    i = pl.program_id(0)
    nb = nb_ref[0]
    slot = i % 2
    rows = xs.shape[1]

    @pl.when(i == 0)
    def _():
        _gather_start(idx_cur, h_hbm, xs, 0, sem, rows)

    @pl.when(i + 1 < nb)
    def _():
        _gather_start(idx_nxt, h_hbm, xs, 1 - slot, sem, rows)

    @pl.when(i < nb)
    def _():
        _gather_wait(h_hbm, xs, slot, sem, rows)
        x = xs[slot].astype(BF16)
        g = _dot(x, wg_ref[0])
        u = _dot(x, wu_ref[0])
        y_ref[...] = _dot((g * _sigmoid(g) * u).astype(BF16), wd_ref[0])

    @pl.when(i >= nb)
    def _():
        y_ref[...] = jnp.zeros(y_ref.shape, F32)


def _expert_call(block_expert, n_used, slot_token, h2, wg, wu, wd):
    n_blocks = slot_token.shape[0]
    idx_spec = lambda f: pl.BlockSpec((1, 1, MOE_ROWS), f, memory_space=pltpu.SMEM)
    grid_spec = pltpu.PrefetchScalarGridSpec(
        num_scalar_prefetch=2,
        grid=(n_blocks,),
        in_specs=[idx_spec(lambda i, be, nb: (i, 0, 0)),
                  idx_spec(lambda i, be, nb: (jnp.minimum(i + 1, n_blocks - 1), 0, 0)),
                  pl.BlockSpec(memory_space=pl.ANY),
                  pl.BlockSpec((1, D, D_EXPERT), lambda i, be, nb: (be[i], 0, 0)),
                  pl.BlockSpec((1, D, D_EXPERT), lambda i, be, nb: (be[i], 0, 0)),
                  pl.BlockSpec((1, D_EXPERT, D), lambda i, be, nb: (be[i], 0, 0))],
        out_specs=pl.BlockSpec((MOE_ROWS, D), lambda i, be, nb: (i, 0)),
        scratch_shapes=[pltpu.VMEM((2, MOE_ROWS, D), F32), pltpu.SemaphoreType.DMA((2,))],
    )
    return pl.pallas_call(
        _expert_body,
        grid_spec=grid_spec,
        out_shape=jax.ShapeDtypeStruct((n_blocks * MOE_ROWS, D), F32),
        compiler_params=_params(("arbitrary",), 56),
        name="moe_experts",
    )(block_expert, n_used, slot_token, slot_token, h2, wg, wu, wd)


def _combine_body(d_cur, d_nxt, ys_hbm, x_ref, g2_ref, wt_ref, gf_ref, o_ref, buf, sem, *, n_tiles):
    i = pl.program_id(0)
    slot = i % 2
    tt = x_ref.shape[0]

    @pl.when(i == 0)
    def _():
        _gather_start(d_cur, ys_hbm, buf, 0, sem, 2 * tt)

    @pl.when(i + 1 < n_tiles)
    def _():
        _gather_start(d_nxt, ys_hbm, buf, 1 - slot, sem, 2 * tt)

    _gather_wait(ys_hbm, buf, slot, sem, 2 * tt)
    wt = wt_ref[...]
    ffn = wt[:, 0:1] * buf[slot, 0:tt, :] + wt[:, 1:2] * buf[slot, tt:2 * tt, :]
    x = x_ref[...] + g2_ref[0] * ffn
    o_ref[...] = _rms(x, gf_ref[...])


def _combine_call(dest_tiles, ys, x, g2, rwt, g_final, *, rows_per_mod, tile_off):
    tt = COMB_TILE
    m = x.shape[0]
    n_tiles = m // tt
    r = g2.shape[1]
    idx_spec = lambda f: pl.BlockSpec((1, 1, 2 * tt), f, memory_space=pltpu.SMEM)
    return pl.pallas_call(
        functools.partial(_combine_body, n_tiles=n_tiles),
        grid=(n_tiles,),
        in_specs=[idx_spec(lambda i: (i + tile_off, 0, 0)),
                  idx_spec(lambda i: (jnp.minimum(i + 1, n_tiles - 1) + tile_off, 0, 0)),
                  pl.BlockSpec(memory_space=pl.ANY),
                  pl.BlockSpec((tt, D), lambda i: (i, 0)),
                  pl.BlockSpec((1, r, D), lambda i: (i * tt // rows_per_mod, 0, 0)),
                  pl.BlockSpec((tt, LANES), lambda i: (i + tile_off, 0)),
                  pl.BlockSpec((1, D), lambda i: (0, 0))],
        out_specs=pl.BlockSpec((tt, D), lambda i: (i, 0)),
        out_shape=jax.ShapeDtypeStruct((m, D), F32),
        scratch_shapes=[pltpu.VMEM((2, 2 * tt, D), F32), pltpu.SemaphoreType.DMA((2,))],
        compiler_params=_params(("arbitrary",), 40),
        name="moe_combine",
    )(dest_tiles, dest_tiles, ys, x, g2, rwt, g_final)


def kernel(x_prompt, x_sample, cache_k, cache_v, state_conv, page_table, c_prompt, c_sample, w_ada, b_ada, g_mix, w_in, lambda_q1, lambda_k1, lambda_q2, lambda_k2, g_head, w_attn_proj, w_dw, b_dw, g_conv_ln, b_conv_ln, w_conv_proj, w_out, g_ffn, w_group_router, b_group_router, w_expert_router, b_expert_router, w_gate_e, w_up_e, w_down_e, g_final):
    assert w_in.shape[0] == 1, "single-layer step"
    bp, seq, _ = x_prompt.shape
    bs = x_sample.shape[0]
    assert x_sample.shape[1] == 1
    tp = bp * seq
    n_pool, page = cache_k.shape[1], cache_k.shape[2]

    row = lambda a: a.reshape(1, -1)
    lam = (jnp.exp(jnp.sum(lambda_q1[0] * lambda_k1[0])) - jnp.exp(jnp.sum(lambda_q2[0] * lambda_k2[0]))
           + LAM_INIT).reshape(1).astype(F32)
    xp = x_prompt.reshape(tp, D)
    xs = x_sample.reshape(bs, D)

    c_all = jnp.concatenate([c_prompt, c_sample], axis=0)
    (mod,) = _mm_call([c_all], [(w_ada[0], 0)], [(0, 0)],
                      [(row(b_ada[0]), (1, 512), lambda i, j: (0, j))],
                      lambda accs, ex: [accs[0] + ex[0]], [F32], name="adaln_mod", tm=c_all.shape[0], tn=512, n_col=N_MOD * D // 512,
                      prologue=lambda lhs, ex: [lhs[0] * _sigmoid(lhs[0])])
    mod_p = [mod[:bp, k * D:(k + 1) * D].reshape(bp, 1, D) for k in range(N_MOD)]
    mod_s = [mod[bp:, k * D:(k + 1) * D] for k in range(N_MOD)]

    w_in_b = w_in[0].astype(BF16)
    w_attn_b = w_attn_proj[0].astype(BF16)
    w_conv_b = w_conv_proj[0].astype(BF16)
    w_out_b = w_out[0].astype(BF16)

    full = lambda i, j: (0, 0)
    (proj_s,) = _mm_call([xs], [(w_in_b, 0)], [(0, 0)],
                         [(row(g_mix[0]), (1, D), full), (mod_s[1], (bs, D), full), (mod_s[0], (bs, D), full)],
                         lambda accs, ex: [accs[0]], [F32], name="sample_inproj", tm=bs, tn=512, n_col=N_IN // 512,
                         prologue=lambda lhs, ex: [_rms(lhs[0], ex[0]) * (1.0 + ex[1]) + ex[2]])
    seg = lambda k: proj_s[:, k * D:(k + 1) * D]
    q_s, k_s, v_s, uval_s, ugate_s, agate_s, cgate_s = (seg(k) for k in range(7))
    o_s = _decode_call(page_table, lam, q_s.reshape(bs, 1, D), k_s.reshape(bs, 1, D), v_s.reshape(bs, 1, D),
                       row(g_head[0]), cache_k.reshape(n_pool * page, N_MAPS, HD),
                       cache_v.reshape(n_pool * page, H, 2, LANES).transpose(0, 2, 1, 3).reshape(n_pool * page, 2 * H, LANES),
                       n_pool=n_pool, n_pg=4)
    ident = lambda accs, ex: [accs[0]]
    (yattn_s,) = _mm_call([o_s.reshape(bs, D)], [(w_attn_b, 0)], [(0, 0)], [], ident, [F32],
                          name="sample_attnproj", tm=bs, tn=512, n_col=D // 512)
    w_dw_pad = jnp.pad(w_dw[0], ((0, CONV_HALO - CONV_W), (0, 0)))
    u_s, z_s = _sconv_call(uval_s, ugate_s, jnp.swapaxes(state_conv[0], 0, 1), w_dw_pad, row(b_dw[0]),
                           row(g_conv_ln[0]), row(b_conv_ln[0]))
    (yconv_s,) = _mm_call([z_s], [(w_conv_b, 0)], [(0, 0)], [], ident, [F32],
                          name="sample_convproj", tm=bs, tn=512, n_col=D // 512)
    tile_j = lambda i, j: (0, j)
    (x1_s,) = _mm_call([yattn_s, yconv_s], [(w_out_b, 0)], [(0, 0)],
                       [(agate_s, (bs, D), full), (cgate_s, (bs, D), full), (xs, (bs, 512), tile_j),
                        (mod_s[2], (bs, 512), tile_j)],
                       lambda accs, ex: [ex[2] + ex[3] * accs[0]], [F32], name="sample_outproj", tm=bs, tn=512, n_col=D // 512,
                       prologue=lambda lhs, ex: [_sigmoid(ex[0]) * lhs[0] + _sigmoid(ex[1]) * lhs[1]])

    h_p = _normmod_call(xp, row(g_mix[0]), mod_p[1], mod_p[0], tm=512, rows_per_mod=seq, out_dtype=BF16)
    mm = functools.partial(_mm_call, [h_p])
    col = D // 1024
    (q_p,) = mm([(w_in_b, 0)], [(0, 0)], [], ident, [BF16], name="prompt_q", tm=1024, tn=1024, n_col=col)
    k_p = _kproj_call(h_p, w_in_b, col, tm=1024)
    (v_p,) = mm([(w_in_b, 2 * col)], [(0, 0)], [], ident, [F32], name="prompt_v", tm=1024, tn=1024, n_col=col)
    col5 = D // 512
    (u_p,) = mm([(w_in_b, 3 * col5), (w_in_b, 4 * col5)], [(0, 0), (0, 1)], [],
                lambda accs, ex: [accs[0] * _sigmoid(accs[1])], [F32], name="prompt_glu", tm=1024, tn=512, n_col=col5)
    (gates_p,) = mm([(w_in_b, 5 * col)], [(0, 0)], [], lambda accs, ex: [_sigmoid(accs[0])], [BF16],
                    name="prompt_gates", tm=1024, tn=1024, n_col=2 * col)
    o_p = _attn_call(lam, q_p, k_p, v_p, row(g_head[0]), batch=bp, seq=seq, tq=512)
    z_p = _conv_call(u_p.reshape(bp, seq, D), jnp.broadcast_to(w_dw_pad[:, None, :], (CONV_HALO, SUBLANES, D)),
                     jnp.broadcast_to(row(b_dw[0]), (SUBLANES, D)), row(g_conv_ln[0]), row(b_conv_ln[0]),
                     tt=256).reshape(tp, D)
    (merged_p,) = _mm_call([o_p, z_p], [(w_attn_b, 0), (w_conv_b, 0)],
                           [(0, 0), (1, 1)],
                           [(gates_p, (1024, 512), lambda i, j: (i, j)),
                            (gates_p, (1024, 512), lambda i, j: (i, j + col5))],
                           lambda accs, ex: [ex[0].astype(F32) * accs[0] + ex[1].astype(F32) * accs[1]], [BF16],
                           name="prompt_merge", tm=1024, tn=512, n_col=col5)
    (x1_p,) = _mm_call([merged_p], [(w_out_b, 0)], [(0, 0)],
                       [(xp, (1024, 1024), lambda i, j: (i, j)),
                        (mod_p[2], (1, 1, 1024), lambda i, j: (i * 1024 // seq, 0, j))],
                       lambda accs, ex: [ex[0] + ex[1][0] * accs[0]], [F32], name="prompt_outproj", tm=1024, tn=1024,
                       n_col=col)

    n_joint = tp + TOK_TILE
    wr = jnp.pad(jnp.concatenate([w_group_router[0], w_expert_router[0]], axis=1),
                 ((0, 0), (0, LANES - N_GROUPS - N_EXPERTS)))
    br = jnp.pad(jnp.concatenate([b_group_router[0], b_expert_router[0]]), (0, LANES - N_GROUPS - N_EXPERTS))
    pad_rows = lambda a, n: jnp.pad(a, ((0, n - a.shape[0]), (0, 0)))
    h2, rid, rwt = _router_call(x1_p, pad_rows(x1_s, TOK_TILE), row(g_ffn[0]), mod_p[4], mod_p[3],
                                pad_rows(mod_s[4], TOK_TILE)[None], pad_rows(mod_s[3], TOK_TILE)[None], wr, row(br),
                                seq=seq, n_sample=bs)
    rank, cnt = _rank_call(rid)

    n_pairs = 2 * (tp + bs)
    n_blocks = -(-n_pairs // MOE_ROWS) + N_EXPERTS
    counts = cnt[0, :N_EXPERTS].astype(I32)
    padded = (counts + MOE_ROWS - 1) // MOE_ROWS * MOE_ROWS
    pad_end = jnp.cumsum(padded)
    pad_start = pad_end - padded
    n_used = (pad_end[-1] // MOE_ROWS).astype(I32)
    expert = rid[:, :2]
    live = expert >= 0
    dest = jnp.where(live, pad_start[jnp.clip(expert, 0, N_EXPERTS - 1)] + rank[:, :2], 0)
    token = jnp.broadcast_to(jnp.arange(n_joint, dtype=I32)[:, None], dest.shape)
    slot_token = jnp.zeros((n_blocks * MOE_ROWS,), I32).at[jnp.where(live, dest, n_blocks * MOE_ROWS)].set(
        token, mode="drop")
    blk = jnp.arange(n_blocks, dtype=I32)
    first_row = jnp.minimum(blk, n_used - 1) * MOE_ROWS
    block_expert = jnp.minimum(jnp.sum((pad_end[None, :] <= first_row[:, None]).astype(I32), axis=1), N_EXPERTS - 1)
    ys = _expert_call(block_expert, n_used.reshape(1), slot_token.reshape(n_blocks, 1, MOE_ROWS), h2,
                      w_gate_e[0].astype(BF16), w_up_e[0].astype(BF16), w_down_e[0].astype(BF16))

    dest_tiles = dest.reshape(n_joint // COMB_TILE, COMB_TILE, 2).transpose(0, 2, 1).reshape(
        n_joint // COMB_TILE, 1, 2 * COMB_TILE)
    y_p = _combine_call(dest_tiles, ys, x1_p, mod_p[5], rwt, row(g_final), rows_per_mod=seq, tile_off=0)
    y_s = _combine_call(dest_tiles, ys, pad_rows(x1_s, COMB_TILE), pad_rows(mod_s[5], COMB_TILE)[None], rwt,
                        row(g_final), rows_per_mod=COMB_TILE, tile_off=tp // COMB_TILE)[:bs]

    conv_p = u_p.reshape(bp, seq, D)[:, seq - HIST:, :]
    conv_s = jnp.concatenate([state_conv[0][:, 1:, :], u_s[:, None, :]], axis=1)
    return (y_p.reshape(bp, seq, D), y_s.reshape(bs, 1, D),
            k_p.reshape(1, bp, seq, H, 2, HD), v_p.reshape(1, bp, seq, H, VD), conv_p[None],
            k_s.reshape(1, bs, 1, H, 2, HD), v_s.reshape(1, bs, 1, H, VD), conv_s[None])
```

```python
import functools
import math

import jax
import jax.numpy as jnp
from jax import lax
from jax.experimental import pallas as pl
from jax.experimental.pallas import tpu as pltpu

F32 = jnp.float32
BF16 = jnp.bfloat16
I32 = jnp.int32

D = 2048
H = 8
HD = 128
VD = 2 * HD
N_MAPS = 2 * H
CONV_W = 31
HIST = CONV_W - 1
N_GROUPS = 4
EPG = 8
N_EXPERTS = N_GROUPS * EPG
D_EXPERT = D // 2
N_MOD = 6
N_IN = 7 * D
EPS = 1e-6
HEAD_EPS = 1e-5
LAM_INIT = 0.8 - 0.6 * math.exp(0.0)
NEG = -1e30
LANES = 128
SUBLANES = 8
MIB = 1024 * 1024

MOE_ROWS = 256
TOK_TILE = 256
COMB_TILE = 128


def _params(sem, vmem_mb):
    return pltpu.CompilerParams(dimension_semantics=sem, vmem_limit_bytes=vmem_mb * MIB)


def _dot(a, b):
    return jnp.dot(a.astype(BF16), b.astype(BF16), preferred_element_type=F32)


def _sigmoid(x):
    return 1.0 / (1.0 + jnp.exp(-x))


def _rms(x, g):
    return x * lax.rsqrt(jnp.mean(x * x, axis=-1, keepdims=True) + EPS) * g


def _mm_body(*refs, n_lhs, n_w, n_ex, pairs, prologue, epilogue):
    lhs = [r[...] for r in refs[:n_lhs]]
    w_refs = refs[n_lhs:n_lhs + n_w]
    ex = [r[...] for r in refs[n_lhs + n_w:n_lhs + n_w + n_ex]]
    outs = refs[n_lhs + n_w + n_ex:]
    if prologue is not None:
        lhs = prologue(lhs, ex)
    accs = []
    for li, wi in pairs:
        accs.append(_dot(lhs[li], w_refs[wi][...]))
    for r, o in zip(outs, epilogue(accs, ex)):
        r[...] = o.astype(r.dtype)


def _mm_call(lhs, ws, pairs, extras, epilogue, out_dtypes, *, name, tm, tn, n_col, prologue=None, vmem_mb=48):
    m = lhs[0].shape[0]
    in_specs = [pl.BlockSpec((tm, a.shape[1]), lambda i, j: (i, 0)) for a in lhs]
    in_specs += [pl.BlockSpec((w.shape[0], tn), functools.partial(lambda i, j, off: (0, j + off), off=off))
                 for w, off in ws]
    in_specs += [pl.BlockSpec(blk, imap) for _, blk, imap in extras]
    body = functools.partial(_mm_body, n_lhs=len(lhs), n_w=len(ws), n_ex=len(extras), pairs=pairs,
                             prologue=prologue, epilogue=epilogue)
    outs = pl.pallas_call(
        body,
        grid=(m // tm, n_col),
        in_specs=in_specs,
        out_specs=[pl.BlockSpec((tm, tn), lambda i, j: (i, j)) for _ in out_dtypes],
        out_shape=[jax.ShapeDtypeStruct((m, n_col * tn), dt) for dt in out_dtypes],
        compiler_params=_params(("parallel", "arbitrary"), vmem_mb),
        name=name,
    )(*lhs, *[w for w, _ in ws], *[a for a, _, _ in extras])
    return outs


def _normmod_body(x_ref, g_ref, sc_ref, sh_ref, o_ref):
    y = _rms(x_ref[...], g_ref[...])
    o_ref[...] = (y * (1.0 + sc_ref[0]) + sh_ref[0]).astype(o_ref.dtype)


def _normmod_call(x, g, sc, sh, *, tm, rows_per_mod, out_dtype):
    m = x.shape[0]
    r = sc.shape[1]
    mod_spec = pl.BlockSpec((1, r, D), lambda i: (i * tm // rows_per_mod, 0, 0))
    return pl.pallas_call(
        _normmod_body,
        grid=(m // tm,),
        in_specs=[pl.BlockSpec((tm, D), lambda i: (i, 0)), pl.BlockSpec((1, D), lambda i: (0, 0)), mod_spec, mod_spec],
        out_specs=pl.BlockSpec((tm, D), lambda i: (i, 0)),
        out_shape=jax.ShapeDtypeStruct((m, D), out_dtype),
        compiler_params=_params(("parallel",), 40),
        name="normmod",
    )(x, g, sc, sh)


def _strided_rows(ref, first, n, stride):
    flat = ref.reshape(ref.shape[0] * ref.shape[1], LANES)
    return flat[pl.ds(first, n, stride=stride), :]


def _store_strided_rows(ref, first, stride, val):
    flat = ref.reshape(ref.shape[0] * ref.shape[1], LANES)
    flat[pl.ds(first, val.shape[0], stride=stride), :] = val


def _kproj_body(h_ref, w_ref, o_ref):
    acc = _dot(h_ref[...], w_ref[...])
    for c in range(o_ref.shape[1]):
        _store_strided_rows(o_ref, c, o_ref.shape[1], acc[:, c * HD:(c + 1) * HD])


def _kproj_call(h, w, col_off, *, tm):
    m = h.shape[0]
    half = N_MAPS // 2
    return pl.pallas_call(
        _kproj_body,
        grid=(m // tm, 2),
        in_specs=[pl.BlockSpec((tm, D), lambda i, j: (i, 0)),
                  pl.BlockSpec((D, half * HD), lambda i, j: (0, j + col_off))],
        out_specs=pl.BlockSpec((tm, half, HD), lambda i, j: (i, j, 0)),
        out_shape=jax.ShapeDtypeStruct((m, N_MAPS, HD), F32),
        compiler_params=_params(("parallel", "arbitrary"), 48),
        name="prompt_k",
    )(h, w)


def _attn_body(lam_ref, q_ref, k_ref, v_ref, gh_ref, o_ref, kb, vb, *, tq, scale, hpb):
    qi = pl.program_id(2)
    head = pl.program_id(1)

    @pl.when(qi == 0)
    def _():
        for mp in range(2 * hpb):
            kb[:, mp * HD:(mp + 1) * HD] = _strided_rows(k_ref, 2 * hpb * head + mp, k_ref.shape[0],
                                                         N_MAPS).astype(BF16)
        vb[...] = v_ref[...].astype(BF16)

    q = q_ref[...]
    lam = lam_ref[0]
    nt = (((1,), (1,)), ((), ()))

    def step(kc, carry, masked):
        off = pl.multiple_of(kc * tq, tq)
        ks = kb[pl.ds(off, tq), :]
        vs = vb[pl.ds(off, tq), :]
        new = []
        for mp in range(2 * hpb):
            m, l, a = carry[mp]
            s = lax.dot_general(q[:, mp * HD:(mp + 1) * HD], ks[:, mp * HD:(mp + 1) * HD], nt,
                                preferred_element_type=F32) * scale
            if masked:
                row = lax.broadcasted_iota(I32, (tq, tq), 0)
                col = lax.broadcasted_iota(I32, (tq, tq), 1)
                s = jnp.where(col <= row, s, NEG)
            mn = jnp.maximum(m, jnp.max(s, axis=1, keepdims=True))
            alpha = jnp.exp(m - mn)
            p = jnp.exp(s - mn)
            l = alpha * l + jnp.sum(p, axis=1, keepdims=True)
            a = alpha * a + _dot(p.astype(BF16), vs[:, (mp // 2) * VD:(mp // 2 + 1) * VD])
            new.append((mn, l, a))
        return tuple(new)

    init = tuple((jnp.full((tq, 1), NEG, F32), jnp.zeros((tq, 1), F32), jnp.zeros((tq, VD), F32))
                 for _ in range(2 * hpb))
    carry = lax.fori_loop(0, qi, lambda kc, c: step(kc, c, False), init)
    fin = step(qi, carry, True)
    for hh in range(hpb):
        (_, l0, a0), (_, l1, a1) = fin[2 * hh], fin[2 * hh + 1]
        o = a0 / l0 - lam * (a1 / l1)
        o = o * lax.rsqrt(jnp.mean(o * o, axis=1, keepdims=True) + HEAD_EPS) * gh_ref[...] * (1.0 - LAM_INIT)
        o_ref[:, hh * VD:(hh + 1) * VD] = o.astype(o_ref.dtype)


def _attn_call(lam, q, k, v, g_head, *, batch, seq, tq, hpb):
    nq = seq // tq
    body = functools.partial(_attn_body, tq=tq, scale=HD ** -0.5, hpb=hpb)
    return pl.pallas_call(
        body,
        grid=(batch, H // hpb, nq),
        in_specs=[
            pl.BlockSpec(memory_space=pltpu.SMEM),
            pl.BlockSpec((tq, hpb * VD), lambda b, h, i: (b * nq + i, h)),
            pl.BlockSpec((seq, N_MAPS, HD), lambda b, h, i: (b, 0, 0), pipeline_mode=pl.Buffered(1)),
            pl.BlockSpec((seq, hpb * VD), lambda b, h, i: (b, h)),
            pl.BlockSpec((1, VD), lambda b, h, i: (0, 0)),
        ],
        out_specs=pl.BlockSpec((tq, hpb * VD), lambda b, h, i: (b * nq + i, h)),
        out_shape=jax.ShapeDtypeStruct((batch * seq, D), BF16),
        scratch_shapes=[pltpu.VMEM((seq, hpb * VD), BF16), pltpu.VMEM((seq, hpb * VD), BF16)],
        compiler_params=_params(("parallel", "arbitrary", "arbitrary"), 56),
        name="prompt_attn",
    )(lam, q, k, v, g_head)


def _page_columns(ref):
    keys = ref.shape[0]
    flat = ref.reshape(keys * 8, LANES)
    return [flat[pl.ds(r, keys, stride=8), :] for r in range(8)]


def _k_page(lo_ref, hi_ref):
    return jnp.concatenate(_page_columns(lo_ref) + _page_columns(hi_ref), axis=1).astype(BF16)


def _v_page(left_ref, right_ref):
    left, right = _page_columns(left_ref), _page_columns(right_ref)
    return jnp.concatenate([part for h in range(H) for part in (left[h], right[h])], axis=1).astype(BF16)


def _decode_body(pt_ref, lam_ref, q_ref, kn_ref, vn_ref, gh_ref, *rest, n_pg, scale):
    k_refs = rest[:2 * n_pg]
    v_refs = rest[2 * n_pg:4 * n_pg]
    o_ref = rest[4 * n_pg]
    qb, m_s, l_s, acc_s = rest[4 * n_pg + 1:]
    j = pl.program_id(1)
    nt = (((1,), (1,)), ((), ()))

    @pl.when(j == 0)
    def _():
        row = lax.broadcasted_iota(I32, (N_MAPS, D), 0)
        col = lax.broadcasted_iota(I32, (N_MAPS, D), 1)
        qb[...] = jnp.where(col // HD == row, jnp.broadcast_to(q_ref[0], (N_MAPS, D)), 0.0).astype(BF16)
        m_s[...] = jnp.full((N_MAPS, 1), NEG, F32)
        l_s[...] = jnp.zeros((N_MAPS, 1), F32)
        acc_s[...] = jnp.zeros((N_MAPS, D), F32)

    qv = qb[...]
    s = jnp.concatenate(
        [lax.dot_general(qv, _k_page(k_refs[2 * g], k_refs[2 * g + 1]), nt, preferred_element_type=F32)
         for g in range(n_pg)], axis=1) * scale
    m_old = m_s[...]
    m_new = jnp.maximum(m_old, jnp.max(s, axis=1, keepdims=True))
    alpha = jnp.exp(m_old - m_new)
    p = jnp.exp(s - m_new)
    l_s[...] = alpha * l_s[...] + jnp.sum(p, axis=1, keepdims=True)
    pb = p.astype(BF16)
    pv = _dot(pb[:, 0:LANES], _v_page(v_refs[0], v_refs[1]))
    for g in range(1, n_pg):
        pv = pv + _dot(pb[:, g * LANES:(g + 1) * LANES], _v_page(v_refs[2 * g], v_refs[2 * g + 1]))
    acc_s[...] = alpha * acc_s[...] + pv
    m_s[...] = m_new

    @pl.when(j == pl.num_programs(1) - 1)
    def _():
        qf = qb[...].astype(F32)
        kn = kn_ref[0].astype(BF16).astype(F32)
        vn = vn_ref[0].astype(BF16).astype(F32)
        s_new = jnp.sum(qf * kn, axis=1, keepdims=True) * scale
        m_fin = jnp.maximum(m_s[...], s_new)
        a = jnp.exp(m_s[...] - m_fin)
        p_new = jnp.exp(s_new - m_fin)
        l_fin = l_s[...] * a + p_new
        accn = (acc_s[...] * a + p_new.astype(BF16).astype(F32) * vn) / l_fin
        lam = lam_ref[0]
        for h in range(H):
            cs = slice(h * VD, (h + 1) * VD)
            o = accn[2 * h:2 * h + 1, cs] - lam * accn[2 * h + 1:2 * h + 2, cs]
            o = o * lax.rsqrt(jnp.mean(o * o, axis=1, keepdims=True) + HEAD_EPS) * gh_ref[...] * (1.0 - LAM_INIT)
            o_ref[0, :, cs] = o


def _decode_call(page_table, lam, q, k_new, v_new, g_head, cache_k, cache_v, *, n_pool, n_pg):
    b, n_pages = page_table.shape
    page = cache_k.shape[0] // n_pool
    row_spec = pl.BlockSpec((1, 1, D), lambda bi, j, pt: (bi, 0, 0))
    page_specs = [pl.BlockSpec((page, 8, LANES),
                               functools.partial(lambda bi, j, pt, g, half: (pt[bi, j * n_pg + g], half, 0), g=g, half=half))
                  for g in range(n_pg) for half in range(2)]
    body = functools.partial(_decode_body, n_pg=n_pg, scale=HD ** -0.5)
    grid_spec = pltpu.PrefetchScalarGridSpec(
        num_scalar_prefetch=1,
        grid=(b, n_pages // n_pg),
        in_specs=[pl.BlockSpec(memory_space=pltpu.SMEM), row_spec, row_spec, row_spec,
                  pl.BlockSpec((1, VD), lambda bi, j, pt: (0, 0))] + page_specs + page_specs,
        out_specs=pl.BlockSpec((1, 1, D), lambda bi, j, pt: (bi, 0, 0)),
        scratch_shapes=[pltpu.VMEM((N_MAPS, D), BF16), pltpu.VMEM((N_MAPS, 1), F32), pltpu.VMEM((N_MAPS, 1), F32),
                        pltpu.VMEM((N_MAPS, D), F32)],
    )
    return pl.pallas_call(
        body,
        grid_spec=grid_spec,
        out_shape=jax.ShapeDtypeStruct((b, 1, D), F32),
        compiler_params=_params(("parallel", "arbitrary"), 56),
        name="decode_attn",
    )(page_table, lam, q, k_new, v_new, g_head, *([cache_k] * (2 * n_pg)), *([cache_v] * (2 * n_pg)))


CONV_HALO = 32
CONV_RC = 32
CONV_CB = 512


def _ln_swish(acc, g, b):
    xc = acc - jnp.mean(acc, axis=1, keepdims=True)
    y = xc * lax.rsqrt(jnp.mean(xc * xc, axis=1, keepdims=True) + EPS) * g + b
    return y * _sigmoid(y)


def _conv_body(cur_ref, prev_ref, w_ref, bdw_ref, g_ref, b_ref, z_ref, full, ph, uc, *, tt):
    t = pl.program_id(1)
    full[0:CONV_HALO, :] = jnp.where(t > 0, prev_ref[0], 0.0).astype(BF16).astype(F32)
    full[CONV_HALO:, :] = cur_ref[0].astype(BF16).astype(F32)
    rows = ph.shape[1]
    for s in range(1, SUBLANES):
        for cb in range(D // CONV_CB):
            cs = slice(cb * CONV_CB, (cb + 1) * CONV_CB)
            ph[s - 1, :, cs] = full[s:s + rows, cs]
    first = CONV_HALO - HIST

    def chunk(c, _):
        r0 = pl.multiple_of(c * CONV_RC, CONV_RC)
        for cb in range(D // CONV_CB):
            cs = slice(cb * CONV_CB, (cb + 1) * CONV_CB)
            accs = [bdw_ref[:, cs] for _ in range(CONV_RC // SUBLANES)]
            for j in range(CONV_W):
                a, s = divmod(first + j, SUBLANES)
                src = full if s == 0 else ph.at[s - 1]
                wj = w_ref[j, :, cs]
                for rg in range(CONV_RC // SUBLANES):
                    accs[rg] = accs[rg] + src[pl.ds(r0 + SUBLANES * (a + rg), SUBLANES), cs] * wj
            for rg in range(CONV_RC // SUBLANES):
                uc[rg * SUBLANES:(rg + 1) * SUBLANES, cs] = accs[rg]
        z_ref[0, pl.ds(r0, CONV_RC), :] = _ln_swish(uc[...], g_ref[...], b_ref[...]).astype(z_ref.dtype)
        return 0

    lax.fori_loop(0, tt // CONV_RC, chunk, 0)


def _conv_call(u, w_dw, b_dw, g_cln, b_cln, *, tt):
    b, s, _ = u.shape
    ratio = tt // CONV_HALO
    vec = pl.BlockSpec((1, D), lambda bi, t: (0, 0))
    return pl.pallas_call(
        functools.partial(_conv_body, tt=tt),
        grid=(b, s // tt),
        in_specs=[pl.BlockSpec((1, tt, D), lambda bi, t: (bi, t, 0)),
                  pl.BlockSpec((1, CONV_HALO, D), lambda bi, t: (bi, jnp.maximum(t * ratio - 1, 0), 0)),
                  pl.BlockSpec((CONV_HALO, SUBLANES, D), lambda bi, t: (0, 0, 0)),
                  pl.BlockSpec((SUBLANES, D), lambda bi, t: (0, 0)), vec, vec],
        out_specs=pl.BlockSpec((1, tt, D), lambda bi, t: (bi, t, 0)),
        out_shape=jax.ShapeDtypeStruct((b, s, D), BF16),
        scratch_shapes=[pltpu.VMEM((tt + CONV_HALO, D), F32),
                        pltpu.VMEM((SUBLANES - 1, tt + CONV_HALO - SUBLANES, D), F32),
                        pltpu.VMEM((CONV_RC, D), F32)],
        compiler_params=_params(("parallel", "arbitrary"), 40),
        name="prompt_conv",
    )(u, u, w_dw, b_dw, g_cln, b_cln)


def _sconv_body(uv_ref, ug_ref, st_ref, w_ref, bdw_ref, g_ref, b_ref, u_ref, z_ref):
    u = uv_ref[...] * _sigmoid(ug_ref[...])
    rnd = lambda a: a.astype(BF16).astype(F32)
    acc = bdw_ref[...] + rnd(u) * w_ref[HIST:HIST + 1, :]
    for j in range(HIST):
        acc = acc + rnd(st_ref[j]) * w_ref[j:j + 1, :]
    u_ref[...] = u
    z_ref[...] = _ln_swish(acc, g_ref[...], b_ref[...])


def _sconv_call(u_val, u_gate, state_t, w_dw, b_dw, g_cln, b_cln):
    b = u_val.shape[0]
    return pl.pallas_call(
        _sconv_body,
        out_shape=[jax.ShapeDtypeStruct((b, D), F32), jax.ShapeDtypeStruct((b, D), F32)],
        compiler_params=pltpu.CompilerParams(vmem_limit_bytes=40 * MIB),
        name="sample_conv",
    )(u_val, u_gate, state_t, w_dw, b_dw, g_cln, b_cln)


def _router_body(xp_ref, xs_ref, g_ref, scp_ref, shp_ref, scs_ref, shs_ref, wr_ref, br_ref,
                 h2_ref, rid_ref, rwt_ref, *, n_prompt_tiles, n_sample):
    tm = xp_ref.shape[0]
    is_s = pl.program_id(0) == n_prompt_tiles
    x = jnp.where(is_s, xs_ref[...], xp_ref[...])
    sc = jnp.where(is_s, scs_ref[0], scp_ref[0])
    sh = jnp.where(is_s, shs_ref[0], shp_ref[0])
    n_valid = jnp.where(is_s, n_sample, tm)
    h2 = _rms(x, g_ref[...]) * (1.0 + sc) + sh
    h2_ref[...] = h2
    logits = _dot(h2, wr_ref[...]) + br_ref[...]
    lane = lax.broadcasted_iota(I32, (tm, LANES), 1).astype(F32)
    far = float(LANES)
    gl = jnp.where(lane < N_GROUPS, logits, NEG)
    gmax = jnp.max(gl, axis=1, keepdims=True)
    grp = jnp.min(jnp.where(gl == gmax, lane, far), axis=1, keepdims=True)
    p_grp = 1.0 / jnp.sum(jnp.where(lane < N_GROUPS, jnp.exp(gl - gmax), 0.0), axis=1, keepdims=True)
    lo = N_GROUPS + grp * EPG
    el = jnp.where((lane >= lo) & (lane < lo + EPG), logits, NEG)
    t1 = jnp.max(el, axis=1, keepdims=True)
    i1 = jnp.min(jnp.where(el == t1, lane, far), axis=1, keepdims=True)
    el2 = jnp.where(lane == i1, NEG, el)
    t2 = jnp.max(el2, axis=1, keepdims=True)
    i2 = jnp.min(jnp.where(el2 == t2, lane, far), axis=1, keepdims=True)
    e = jnp.exp(t2 - t1)
    w1 = p_grp / (1.0 + e)
    w2 = p_grp * e / (1.0 + e)
    valid = lax.broadcasted_iota(I32, (tm, LANES), 0) < n_valid
    ids = jnp.where(lane == 0.0, i1 - N_GROUPS, jnp.where(lane == 1.0, i2 - N_GROUPS, -1.0))
    rid_ref[...] = jnp.where(valid, ids, -1.0).astype(I32)
    rwt_ref[...] = jnp.where(lane == 0.0, w1, jnp.where(lane == 1.0, w2, 0.0))


def _router_call(x_p, x_s, g, sc_p, sh_p, sc_s, sh_s, wr, br, *, seq, n_sample):
    tm = TOK_TILE
    npt = x_p.shape[0] // tm
    n_total = x_p.shape[0] + tm
    prow = lambda i: jnp.minimum(i, npt - 1)
    pmod = pl.BlockSpec((1, 1, D), lambda i: (prow(i) * tm // seq, 0, 0))
    smod = pl.BlockSpec((1, tm, D), lambda i: (0, 0, 0))
    return pl.pallas_call(
        functools.partial(_router_body, n_prompt_tiles=npt, n_sample=n_sample),
        grid=(npt + 1,),
        in_specs=[pl.BlockSpec((tm, D), lambda i: (prow(i), 0)), pl.BlockSpec((tm, D), lambda i: (0, 0)),
                  pl.BlockSpec((1, D), lambda i: (0, 0)), pmod, pmod, smod, smod,
                  pl.BlockSpec((D, LANES), lambda i: (0, 0)), pl.BlockSpec((1, LANES), lambda i: (0, 0))],
        out_specs=[pl.BlockSpec((tm, D), lambda i: (i, 0)), pl.BlockSpec((tm, LANES), lambda i: (i, 0)),
                   pl.BlockSpec((tm, LANES), lambda i: (i, 0))],
        out_shape=[jax.ShapeDtypeStruct((n_total, D), F32), jax.ShapeDtypeStruct((n_total, LANES), I32),
                   jax.ShapeDtypeStruct((n_total, LANES), F32)],
        compiler_params=_params(("arbitrary",), 40),
        name="router",
    )(x_p, x_s, g, sc_p, sh_p, sc_s, sh_s, wr, br)


def _rank_body(rid_ref, rank_ref, cnt_ref, carry):
    i = pl.program_id(0)
    tt = rid_ref.shape[0]

    @pl.when(i == 0)
    def _():
        carry[...] = jnp.zeros((1, LANES), F32)

    ids = rid_ref[...]
    lane = lax.broadcasted_iota(I32, (tt, LANES), 1)
    e1 = ids[:, 0:1]
    e2 = ids[:, 1:2]
    hit = jnp.where((lane == e1) | (lane == e2), 1.0, 0.0)
    row = lax.broadcasted_iota(I32, (tt, tt), 0)
    col = lax.broadcasted_iota(I32, (tt, tt), 1)
    before = jnp.where(col < row, 1.0, 0.0).astype(BF16)
    cum = _dot(before, hit.astype(BF16)) + carry[...]
    r1 = jnp.sum(jnp.where(lane == e1, cum, 0.0), axis=1, keepdims=True)
    r2 = jnp.sum(jnp.where(lane == e2, cum, 0.0), axis=1, keepdims=True)
    rank_ref[...] = jnp.where(lane == 0, r1, jnp.where(lane == 1, r2, 0.0)).astype(I32)
    total = carry[...] + jnp.sum(hit, axis=0, keepdims=True)
    carry[...] = total
    cnt_ref[...] = jnp.broadcast_to(total, cnt_ref.shape)


def _rank_call(rid):
    n = rid.shape[0]
    return pl.pallas_call(
        _rank_body,
        grid=(n // TOK_TILE,),
        in_specs=[pl.BlockSpec((TOK_TILE, LANES), lambda i: (i, 0))],
        out_specs=[pl.BlockSpec((TOK_TILE, LANES), lambda i: (i, 0)), pl.BlockSpec((8, LANES), lambda i: (0, 0))],
        out_shape=[jax.ShapeDtypeStruct((n, LANES), I32), jax.ShapeDtypeStruct((8, LANES), F32)],
        scratch_shapes=[pltpu.VMEM((1, LANES), F32)],
        compiler_params=_params(("arbitrary",), 32),
        name="moe_rank",
    )(rid)


def _row_copy(src_hbm, row, dst, slot, r, sem):
    return pltpu.make_async_copy(src_hbm.at[pl.ds(row, 1)], dst.at[slot, pl.ds(r, 1)], sem.at[slot])


def _gather_start(idx_ref, src_hbm, dst, slot, sem, n_rows):
    for r in range(n_rows):
        _row_copy(src_hbm, idx_ref[0, 0, r], dst, slot, r, sem).start()


def _gather_wait(src_hbm, dst, slot, sem, n_rows):
    def body(r, _):
        _row_copy(src_hbm, 0, dst, slot, r, sem).wait()
        return 0
    lax.fori_loop(0, n_rows, body, 0, unroll=8)


def _expert_body(be_ref, nb_ref, idx_cur, idx_nxt, h_hbm, wg_ref, wu_ref, wd_ref, y_ref, xs, sem):
    i = pl.program_id(0)
    nb = nb_ref[0]
    slot = i % 2
    rows = xs.shape[1]

    @pl.when(i == 0)
    def _():
        _gather_start(idx_cur, h_hbm, xs, 0, sem, rows)

    @pl.when(i + 1 < nb)
    def _():
        _gather_start(idx_nxt, h_hbm, xs, 1 - slot, sem, rows)

    @pl.when(i < nb)
    def _():
        _gather_wait(h_hbm, xs, slot, sem, rows)
        x = xs[slot].astype(BF16)
        g = _dot(x, wg_ref[0])
        u = _dot(x, wu_ref[0])
        y_ref[...] = _dot((g * _sigmoid(g) * u).astype(BF16), wd_ref[0])

    @pl.when(i >= nb)
    def _():
        y_ref[...] = jnp.zeros(y_ref.shape, F32)


def _expert_call(block_expert, n_used, slot_token, h2, wg, wu, wd):
    n_blocks = slot_token.shape[0]
    idx_spec = lambda f: pl.BlockSpec((1, 1, MOE_ROWS), f, memory_space=pltpu.SMEM)
    grid_spec = pltpu.PrefetchScalarGridSpec(
        num_scalar_prefetch=2,
        grid=(n_blocks,),
        in_specs=[idx_spec(lambda i, be, nb: (i, 0, 0)),
                  idx_spec(lambda i, be, nb: (jnp.minimum(i + 1, n_blocks - 1), 0, 0)),
                  pl.BlockSpec(memory_space=pl.ANY),
                  pl.BlockSpec((1, D, D_EXPERT), lambda i, be, nb: (be[i], 0, 0)),
                  pl.BlockSpec((1, D, D_EXPERT), lambda i, be, nb: (be[i], 0, 0)),
                  pl.BlockSpec((1, D_EXPERT, D), lambda i, be, nb: (be[i], 0, 0))],
        out_specs=pl.BlockSpec((MOE_ROWS, D), lambda i, be, nb: (i, 0)),
        scratch_shapes=[pltpu.VMEM((2, MOE_ROWS, D), F32), pltpu.SemaphoreType.DMA((2,))],
    )
    return pl.pallas_call(
        _expert_body,
        grid_spec=grid_spec,
        out_shape=jax.ShapeDtypeStruct((n_blocks * MOE_ROWS, D), F32),
        compiler_params=_params(("arbitrary",), 56),
        name="moe_experts",
    )(block_expert, n_used, slot_token, slot_token, h2, wg, wu, wd)


def _combine_body(d_cur, d_nxt, ys_hbm, x_ref, g2_ref, wt_ref, gf_ref, o_ref, buf, sem, *, n_tiles):
    i = pl.program_id(0)
    slot = i % 2
    tt = x_ref.shape[0]

    @pl.when(i == 0)
    def _():
        _gather_start(d_cur, ys_hbm, buf, 0, sem, 2 * tt)

    @pl.when(i + 1 < n_tiles)
    def _():
        _gather_start(d_nxt, ys_hbm, buf, 1 - slot, sem, 2 * tt)

    _gather_wait(ys_hbm, buf, slot, sem, 2 * tt)
    wt = wt_ref[...]
    ffn = wt[:, 0:1] * buf[slot, 0:tt, :] + wt[:, 1:2] * buf[slot, tt:2 * tt, :]
    x = x_ref[...] + g2_ref[0] * ffn
    o_ref[...] = _rms(x, gf_ref[...])


def _combine_call(dest_tiles, ys, x, g2, rwt, g_final, *, rows_per_mod, tile_off):
    tt = COMB_TILE
    m = x.shape[0]
    n_tiles = m // tt
    r = g2.shape[1]
    idx_spec = lambda f: pl.BlockSpec((1, 1, 2 * tt), f, memory_space=pltpu.SMEM)
    return pl.pallas_call(
        functools.partial(_combine_body, n_tiles=n_tiles),
        grid=(n_tiles,),
        in_specs=[idx_spec(lambda i: (i + tile_off, 0, 0)),
                  idx_spec(lambda i: (jnp.minimum(i + 1, n_tiles - 1) + tile_off, 0, 0)),
                  pl.BlockSpec(memory_space=pl.ANY),
                  pl.BlockSpec((tt, D), lambda i: (i, 0)),
                  pl.BlockSpec((1, r, D), lambda i: (i * tt // rows_per_mod, 0, 0)),
                  pl.BlockSpec((tt, LANES), lambda i: (i + tile_off, 0)),
                  pl.BlockSpec((1, D), lambda i: (0, 0))],
        out_specs=pl.BlockSpec((tt, D), lambda i: (i, 0)),
        out_shape=jax.ShapeDtypeStruct((m, D), F32),
        scratch_shapes=[pltpu.VMEM((2, 2 * tt, D), F32), pltpu.SemaphoreType.DMA((2,))],
        compiler_params=_params(("arbitrary",), 40),
        name="moe_combine",
    )(dest_tiles, dest_tiles, ys, x, g2, rwt, g_final)


def kernel(x_prompt, x_sample, cache_k, cache_v, state_conv, page_table, c_prompt, c_sample, w_ada, b_ada, g_mix, w_in, lambda_q1, lambda_k1, lambda_q2, lambda_k2, g_head, w_attn_proj, w_dw, b_dw, g_conv_ln, b_conv_ln, w_conv_proj, w_out, g_ffn, w_group_router, b_group_router, w_expert_router, b_expert_router, w_gate_e, w_up_e, w_down_e, g_final):
    assert w_in.shape[0] == 1, "single-layer step"
    bp, seq, _ = x_prompt.shape
    bs = x_sample.shape[0]
    assert x_sample.shape[1] == 1
    tp = bp * seq
    n_pool, page = cache_k.shape[1], cache_k.shape[2]

    row = lambda a: a.reshape(1, -1)
    lam = (jnp.exp(jnp.sum(lambda_q1[0] * lambda_k1[0])) - jnp.exp(jnp.sum(lambda_q2[0] * lambda_k2[0]))
           + LAM_INIT).reshape(1).astype(F32)
    xp = x_prompt.reshape(tp, D)
    xs = x_sample.reshape(bs, D)

    c_all = jnp.concatenate([c_prompt, c_sample], axis=0)
    (mod,) = _mm_call([c_all], [(w_ada[0], 0)], [(0, 0)],
                      [(row(b_ada[0]), (1, 512), lambda i, j: (0, j))],
                      lambda accs, ex: [accs[0] + ex[0]], [F32], name="adaln_mod", tm=c_all.shape[0], tn=512, n_col=N_MOD * D // 512,
                      prologue=lambda lhs, ex: [lhs[0] * _sigmoid(lhs[0])])
    mod_p = [mod[:bp, k * D:(k + 1) * D].reshape(bp, 1, D) for k in range(N_MOD)]
    mod_s = [mod[bp:, k * D:(k + 1) * D] for k in range(N_MOD)]

    w_in_b = w_in[0].astype(BF16)
    w_attn_b = w_attn_proj[0].astype(BF16)
    w_conv_b = w_conv_proj[0].astype(BF16)
    w_out_b = w_out[0].astype(BF16)

    full = lambda i, j: (0, 0)
    (proj_s,) = _mm_call([xs], [(w_in_b, 0)], [(0, 0)],
                         [(row(g_mix[0]), (1, D), full), (mod_s[1], (bs, D), full), (mod_s[0], (bs, D), full)],
                         lambda accs, ex: [accs[0]], [F32], name="sample_inproj", tm=bs, tn=512, n_col=N_IN // 512,
                         prologue=lambda lhs, ex: [_rms(lhs[0], ex[0]) * (1.0 + ex[1]) + ex[2]])
    seg = lambda k: proj_s[:, k * D:(k + 1) * D]
    q_s, k_s, v_s, uval_s, ugate_s, agate_s, cgate_s = (seg(k) for k in range(7))
    o_s = _decode_call(page_table, lam, q_s.reshape(bs, 1, D), k_s.reshape(bs, 1, D), v_s.reshape(bs, 1, D),
                       row(g_head[0]), cache_k.reshape(n_pool * page, N_MAPS, HD),
                       cache_v.reshape(n_pool * page, H, 2, LANES).transpose(0, 2, 1, 3).reshape(n_pool * page, 2 * H, LANES),
                       n_pool=n_pool, n_pg=4)
    ident = lambda accs, ex: [accs[0]]
    (yattn_s,) = _mm_call([o_s.reshape(bs, D)], [(w_attn_b, 0)], [(0, 0)], [], ident, [F32],
                          name="sample_attnproj", tm=bs, tn=512, n_col=D // 512)
    w_dw_pad = jnp.pad(w_dw[0], ((0, CONV_HALO - CONV_W), (0, 0)))
    u_s, z_s = _sconv_call(uval_s, ugate_s, jnp.swapaxes(state_conv[0], 0, 1), w_dw_pad, row(b_dw[0]),
                           row(g_conv_ln[0]), row(b_conv_ln[0]))
    (yconv_s,) = _mm_call([z_s], [(w_conv_b, 0)], [(0, 0)], [], ident, [F32],
                          name="sample_convproj", tm=bs, tn=512, n_col=D // 512)
    tile_j = lambda i, j: (0, j)
    (x1_s,) = _mm_call([yattn_s, yconv_s], [(w_out_b, 0)], [(0, 0)],
                       [(agate_s, (bs, D), full), (cgate_s, (bs, D), full), (xs, (bs, 512), tile_j),
                        (mod_s[2], (bs, 512), tile_j)],
                       lambda accs, ex: [ex[2] + ex[3] * accs[0]], [F32], name="sample_outproj", tm=bs, tn=512, n_col=D // 512,
                       prologue=lambda lhs, ex: [_sigmoid(ex[0]) * lhs[0] + _sigmoid(ex[1]) * lhs[1]])

    h_p = _normmod_call(xp, row(g_mix[0]), mod_p[1], mod_p[0], tm=512, rows_per_mod=seq, out_dtype=BF16)
    mm = functools.partial(_mm_call, [h_p])
    col = D // 1024
    (q_p,) = mm([(w_in_b, 0)], [(0, 0)], [], ident, [BF16], name="prompt_q", tm=1024, tn=1024, n_col=col)
    k_p = _kproj_call(h_p, w_in_b, col, tm=1024)
    (v_p,) = mm([(w_in_b, 2 * col)], [(0, 0)], [], ident, [F32], name="prompt_v", tm=1024, tn=1024, n_col=col)
    col5 = D // 512
    (u_p,) = mm([(w_in_b, 3 * col5), (w_in_b, 4 * col5)], [(0, 0), (0, 1)], [],
                lambda accs, ex: [accs[0] * _sigmoid(accs[1])], [F32], name="prompt_glu", tm=1024, tn=512, n_col=col5)
    (gates_p,) = mm([(w_in_b, 5 * col)], [(0, 0)], [], lambda accs, ex: [_sigmoid(accs[0])], [BF16],
                    name="prompt_gates", tm=1024, tn=1024, n_col=2 * col)
    o_p = _attn_call(lam, q_p, k_p, v_p, row(g_head[0]), batch=bp, seq=seq, tq=512, hpb=2)
    z_p = _conv_call(u_p.reshape(bp, seq, D), jnp.broadcast_to(w_dw_pad[:, None, :], (CONV_HALO, SUBLANES, D)),
                     jnp.broadcast_to(row(b_dw[0]), (SUBLANES, D)), row(g_conv_ln[0]), row(b_conv_ln[0]),
                     tt=256).reshape(tp, D)
    (merged_p,) = _mm_call([o_p, z_p], [(w_attn_b, 0), (w_conv_b, 0)],
                           [(0, 0), (1, 1)],
                           [(gates_p, (1024, 512), lambda i, j: (i, j)),
                            (gates_p, (1024, 512), lambda i, j: (i, j + col5))],
                           lambda accs, ex: [ex[0].astype(F32) * accs[0] + ex[1].astype(F32) * accs[1]], [BF16],
                           name="prompt_merge", tm=1024, tn=512, n_col=col5)
    (x1_p,) = _mm_call([merged_p], [(w_out_b, 0)], [(0, 0)],
                       [(xp, (1024, 1024), lambda i, j: (i, j)),
                        (mod_p[2], (1, 1, 1024), lambda i, j: (i * 1024 // seq, 0, j))],
                       lambda accs, ex: [ex[0] + ex[1][0] * accs[0]], [F32], name="prompt_outproj", tm=1024, tn=1024,
                       n_col=col)

    n_joint = tp + TOK_TILE
    wr = jnp.pad(jnp.concatenate([w_group_router[0], w_expert_router[0]], axis=1),
                 ((0, 0), (0, LANES - N_GROUPS - N_EXPERTS)))
    br = jnp.pad(jnp.concatenate([b_group_router[0], b_expert_router[0]]), (0, LANES - N_GROUPS - N_EXPERTS))
    pad_rows = lambda a, n: jnp.pad(a, ((0, n - a.shape[0]), (0, 0)))
    h2, rid, rwt = _router_call(x1_p, pad_rows(x1_s, TOK_TILE), row(g_ffn[0]), mod_p[4], mod_p[3],
                                pad_rows(mod_s[4], TOK_TILE)[None], pad_rows(mod_s[3], TOK_TILE)[None], wr, row(br),
                                seq=seq, n_sample=bs)
    rank, cnt = _rank_call(rid)

    n_pairs = 2 * (tp + bs)
    n_blocks = -(-n_pairs // MOE_ROWS) + N_EXPERTS
    counts = cnt[0, :N_EXPERTS].astype(I32)
    padded = (counts + MOE_ROWS - 1) // MOE_ROWS * MOE_ROWS
    pad_end = jnp.cumsum(padded)
    pad_start = pad_end - padded
    n_used = (pad_end[-1] // MOE_ROWS).astype(I32)
    expert = rid[:, :2]
    live = expert >= 0
    dest = jnp.where(live, pad_start[jnp.clip(expert, 0, N_EXPERTS - 1)] + rank[:, :2], 0)
    token = jnp.broadcast_to(jnp.arange(n_joint, dtype=I32)[:, None], dest.shape)
    slot_token = jnp.zeros((n_blocks * MOE_ROWS,), I32).at[jnp.where(live, dest, n_blocks * MOE_ROWS)].set(
        token, mode="drop")
    blk = jnp.arange(n_blocks, dtype=I32)
    first_row = jnp.minimum(blk, n_used - 1) * MOE_ROWS
    block_expert = jnp.minimum(jnp.sum((pad_end[None, :] <= first_row[:, None]).astype(I32), axis=1), N_EXPERTS - 1)
    ys = _expert_call(block_expert, n_used.reshape(1), slot_token.reshape(n_blocks, 1, MOE_ROWS), h2,
                      w_gate_e[0].astype(BF16), w_up_e[0].astype(BF16), w_down_e[0].astype(BF16))

    dest_tiles = dest.reshape(n_joint // COMB_TILE, COMB_TILE, 2).transpose(0, 2, 1).reshape(
        n_joint // COMB_TILE, 1, 2 * COMB_TILE)
    y_p = _combine_call(dest_tiles, ys, x1_p, mod_p[5], rwt, row(g_final), rows_per_mod=seq, tile_off=0)
    y_s = _combine_call(dest_tiles, ys, pad_rows(x1_s, COMB_TILE), pad_rows(mod_s[5], COMB_TILE)[None], rwt,
                        row(g_final), rows_per_mod=COMB_TILE, tile_off=tp // COMB_TILE)[:bs]

    conv_p = u_p.reshape(bp, seq, D)[:, seq - HIST:, :]
    conv_s = jnp.concatenate([state_conv[0][:, 1:, :], u_s[:, None, :]], axis=1)
    return (y_p.reshape(bp, seq, D), y_s.reshape(bs, 1, D),
            k_p.reshape(1, bp, seq, H, 2, HD), v_p.reshape(1, bp, seq, H, VD), conv_p[None],
            k_s.reshape(1, bs, 1, H, 2, HD), v_s.reshape(1, bs, 1, H, VD), conv_s[None])
```

```python
import functools
import math

import jax
import jax.numpy as jnp
from jax import lax
from jax.experimental import pallas as pl
from jax.experimental.pallas import tpu as pltpu

F32 = jnp.float32
BF16 = jnp.bfloat16
I32 = jnp.int32

D = 2048
H = 8
HD = 128
VD = 2 * HD
N_MAPS = 2 * H
CONV_W = 31
HIST = CONV_W - 1
N_GROUPS = 4
EPG = 8
N_EXPERTS = N_GROUPS * EPG
D_EXPERT = D // 2
N_MOD = 6
N_IN = 7 * D
EPS = 1e-6
HEAD_EPS = 1e-5
LAM_INIT = 0.8 - 0.6 * math.exp(0.0)
NEG = -1e30
LANES = 128
SUBLANES = 8
MIB = 1024 * 1024

MOE_ROWS = 256
TOK_TILE = 256
COMB_TILE = 128


def _params(sem, vmem_mb):
    return pltpu.CompilerParams(dimension_semantics=sem, vmem_limit_bytes=vmem_mb * MIB)


def _dot(a, b):
    return jnp.dot(a.astype(BF16), b.astype(BF16), preferred_element_type=F32)


def _sigmoid(x):
    return 1.0 / (1.0 + jnp.exp(-x))


def _rms(x, g):
    return x * lax.rsqrt(jnp.mean(x * x, axis=-1, keepdims=True) + EPS) * g


def _mm_body(*refs, n_lhs, n_w, n_ex, n_out, pairs, prologue, epilogue):
    lhs = [r[...] for r in refs[:n_lhs]]
    w_refs = refs[n_lhs:n_lhs + n_w]
    ex = [r[...] for r in refs[n_lhs + n_w:n_lhs + n_w + n_ex]]
    rest = refs[n_lhs + n_w + n_ex:]
    n_side = (len(rest) - n_out) // 2
    side_in, outs, side_out = rest[:n_side], rest[n_side:n_side + n_out], rest[n_side + n_out:]
    if prologue is not None:
        lhs = prologue(lhs, ex)
    accs = []
    for li, wi in pairs:
        accs.append(_dot(lhs[li], w_refs[wi][...]))
    for r, o in zip(outs, epilogue(accs, ex)):
        r[...] = o.astype(r.dtype)
    for src, dst in zip(side_in, side_out):
        dst[...] = src[...].astype(dst.dtype)


def _mm_call(lhs, ws, pairs, extras, epilogue, out_dtypes, *, name, tm, tn, n_col, prologue=None, side_casts=(),
             vmem_mb=48):
    m = lhs[0].shape[0]
    steps = (m // tm) * n_col
    in_specs = [pl.BlockSpec((tm, a.shape[1]), lambda i, j: (i, 0)) for a in lhs]
    in_specs += [pl.BlockSpec((w.shape[0], tn), functools.partial(lambda i, j, off: (0, j + off), off=off))
                 for w, off in ws]
    in_specs += [pl.BlockSpec(blk, imap) for _, blk, imap in extras]
    side_specs = [pl.BlockSpec((a.shape[0] // steps, a.shape[1]), lambda i, j: (i * n_col + j, 0)) for a in side_casts]
    body = functools.partial(_mm_body, n_lhs=len(lhs), n_w=len(ws), n_ex=len(extras), n_out=len(out_dtypes),
                             pairs=pairs, prologue=prologue, epilogue=epilogue)
    outs = pl.pallas_call(
        body,
        grid=(m // tm, n_col),
        in_specs=in_specs + side_specs,
        out_specs=[pl.BlockSpec((tm, tn), lambda i, j: (i, j)) for _ in out_dtypes] + side_specs,
        out_shape=[jax.ShapeDtypeStruct((m, n_col * tn), dt) for dt in out_dtypes]
        + [jax.ShapeDtypeStruct(a.shape, BF16) for a in side_casts],
        compiler_params=_params(("parallel", "arbitrary"), vmem_mb),
        name=name,
    )(*lhs, *[w for w, _ in ws], *[a for a, _, _ in extras], *side_casts)
    return outs


def _normmod_body(x_ref, g_ref, sc_ref, sh_ref, o_ref):
    y = _rms(x_ref[...], g_ref[...])
    o_ref[...] = (y * (1.0 + sc_ref[0]) + sh_ref[0]).astype(o_ref.dtype)


def _normmod_call(x, g, sc, sh, *, tm, rows_per_mod, out_dtype):
    m = x.shape[0]
    r = sc.shape[1]
    mod_spec = pl.BlockSpec((1, r, D), lambda i: (i * tm // rows_per_mod, 0, 0))
    return pl.pallas_call(
        _normmod_body,
        grid=(m // tm,),
        in_specs=[pl.BlockSpec((tm, D), lambda i: (i, 0)), pl.BlockSpec((1, D), lambda i: (0, 0)), mod_spec, mod_spec],
        out_specs=pl.BlockSpec((tm, D), lambda i: (i, 0)),
        out_shape=jax.ShapeDtypeStruct((m, D), out_dtype),
        compiler_params=_params(("parallel",), 40),
        name="normmod",
    )(x, g, sc, sh)


def _strided_rows(ref, first, n, stride):
    flat = ref.reshape(ref.shape[0] * ref.shape[1], LANES)
    return flat[pl.ds(first, n, stride=stride), :]


def _store_strided_rows(ref, first, stride, val):
    flat = ref.reshape(ref.shape[0] * ref.shape[1], LANES)
    flat[pl.ds(first, val.shape[0], stride=stride), :] = val


def _kproj_body(h_ref, w_ref, o_ref):
    acc = _dot(h_ref[...], w_ref[...])
    for c in range(o_ref.shape[1]):
        _store_strided_rows(o_ref, c, o_ref.shape[1], acc[:, c * HD:(c + 1) * HD])


def _kproj_call(h, w, col_off, *, tm):
    m = h.shape[0]
    half = N_MAPS // 2
    return pl.pallas_call(
        _kproj_body,
        grid=(m // tm, 2),
        in_specs=[pl.BlockSpec((tm, D), lambda i, j: (i, 0)),
                  pl.BlockSpec((D, half * HD), lambda i, j: (0, j + col_off))],
        out_specs=pl.BlockSpec((tm, half, HD), lambda i, j: (i, j, 0)),
        out_shape=jax.ShapeDtypeStruct((m, N_MAPS, HD), F32),
        compiler_params=_params(("parallel", "arbitrary"), 48),
        name="prompt_k",
    )(h, w)


def _attn_body(lam_ref, q_ref, k_ref, v_ref, gh_ref, o_ref, kb, vb, *, tq, scale, hpb):
    qi = pl.program_id(2)
    head = pl.program_id(1)

    @pl.when(qi == 0)
    def _():
        for mp in range(2 * hpb):
            kb[:, mp * HD:(mp + 1) * HD] = _strided_rows(k_ref, 2 * hpb * head + mp, k_ref.shape[0],
                                                         N_MAPS).astype(BF16)
        vb[...] = v_ref[...].astype(BF16)

    q = q_ref[...]
    lam = lam_ref[0]
    nt = (((1,), (1,)), ((), ()))

    def step(kc, carry, masked):
        off = pl.multiple_of(kc * tq, tq)
        ks = kb[pl.ds(off, tq), :]
        vs = vb[pl.ds(off, tq), :]
        new = []
        for mp in range(2 * hpb):
            m, l, a = carry[mp]
            s = lax.dot_general(q[:, mp * HD:(mp + 1) * HD], ks[:, mp * HD:(mp + 1) * HD], nt,
                                preferred_element_type=F32) * scale
            if masked:
                row = lax.broadcasted_iota(I32, (tq, tq), 0)
                col = lax.broadcasted_iota(I32, (tq, tq), 1)
                s = jnp.where(col <= row, s, NEG)
            mn = jnp.maximum(m, jnp.max(s, axis=1, keepdims=True))
            alpha = jnp.exp(m - mn)
            p = jnp.exp(s - mn)
            l = alpha * l + jnp.sum(p, axis=1, keepdims=True)
            a = alpha * a + _dot(p.astype(BF16), vs[:, (mp // 2) * VD:(mp // 2 + 1) * VD])
            new.append((mn, l, a))
        return tuple(new)

    init = tuple((jnp.full((tq, 1), NEG, F32), jnp.zeros((tq, 1), F32), jnp.zeros((tq, VD), F32))
                 for _ in range(2 * hpb))
    carry = lax.fori_loop(0, qi, lambda kc, c: step(kc, c, False), init)
    fin = step(qi, carry, True)
    for hh in range(hpb):
        (_, l0, a0), (_, l1, a1) = fin[2 * hh], fin[2 * hh + 1]
        o = a0 / l0 - lam * (a1 / l1)
        o = o * lax.rsqrt(jnp.mean(o * o, axis=1, keepdims=True) + HEAD_EPS) * gh_ref[...] * (1.0 - LAM_INIT)
        o_ref[:, hh * VD:(hh + 1) * VD] = o.astype(o_ref.dtype)


def _attn_call(lam, q, k, v, g_head, *, batch, seq, tq, hpb):
    nq = seq // tq
    body = functools.partial(_attn_body, tq=tq, scale=HD ** -0.5, hpb=hpb)
    return pl.pallas_call(
        body,
        grid=(batch, H // hpb, nq),
        in_specs=[
            pl.BlockSpec(memory_space=pltpu.SMEM),
            pl.BlockSpec((tq, hpb * VD), lambda b, h, i: (b * nq + i, h)),
            pl.BlockSpec((seq, N_MAPS, HD), lambda b, h, i: (b, 0, 0), pipeline_mode=pl.Buffered(1)),
            pl.BlockSpec((seq, hpb * VD), lambda b, h, i: (b, h)),
            pl.BlockSpec((1, VD), lambda b, h, i: (0, 0)),
        ],
        out_specs=pl.BlockSpec((tq, hpb * VD), lambda b, h, i: (b * nq + i, h)),
        out_shape=jax.ShapeDtypeStruct((batch * seq, D), BF16),
        scratch_shapes=[pltpu.VMEM((seq, hpb * VD), BF16), pltpu.VMEM((seq, hpb * VD), BF16)],
        compiler_params=_params(("parallel", "arbitrary", "arbitrary"), 56),
        name="prompt_attn",
    )(lam, q, k, v, g_head)


def _page_columns(ref):
    keys = ref.shape[0]
    flat = ref.reshape(keys * 8, LANES)
    return [flat[pl.ds(r, keys, stride=8), :] for r in range(8)]


def _k_page(lo_ref, hi_ref):
    return jnp.concatenate(_page_columns(lo_ref) + _page_columns(hi_ref), axis=1).astype(BF16)


def _v_page(left_ref, right_ref):
    left, right = _page_columns(left_ref), _page_columns(right_ref)
    return jnp.concatenate([part for h in range(H) for part in (left[h], right[h])], axis=1).astype(BF16)


def _decode_body(pt_ref, lam_ref, q_ref, kn_ref, vn_ref, gh_ref, *rest, n_pg, scale):
    k_refs = rest[:2 * n_pg]
    v_refs = rest[2 * n_pg:4 * n_pg]
    o_ref = rest[4 * n_pg]
    qb, m_s, l_s, acc_s = rest[4 * n_pg + 1:]
    j = pl.program_id(1)
    nt = (((1,), (1,)), ((), ()))

    @pl.when(j == 0)
    def _():
        row = lax.broadcasted_iota(I32, (N_MAPS, D), 0)
        col = lax.broadcasted_iota(I32, (N_MAPS, D), 1)
        qb[...] = jnp.where(col // HD == row, jnp.broadcast_to(q_ref[0], (N_MAPS, D)), 0.0).astype(BF16)
        m_s[...] = jnp.full((N_MAPS, 1), NEG, F32)
        l_s[...] = jnp.zeros((N_MAPS, 1), F32)
        acc_s[...] = jnp.zeros((N_MAPS, D), F32)

    qv = qb[...]
    s = jnp.concatenate(
        [lax.dot_general(qv, _k_page(k_refs[2 * g], k_refs[2 * g + 1]), nt, preferred_element_type=F32)
         for g in range(n_pg)], axis=1) * scale
    m_old = m_s[...]
    m_new = jnp.maximum(m_old, jnp.max(s, axis=1, keepdims=True))
    alpha = jnp.exp(m_old - m_new)
    p = jnp.exp(s - m_new)
    l_s[...] = alpha * l_s[...] + jnp.sum(p, axis=1, keepdims=True)
    pb = p.astype(BF16)
    pv = _dot(pb[:, 0:LANES], _v_page(v_refs[0], v_refs[1]))
    for g in range(1, n_pg):
        pv = pv + _dot(pb[:, g * LANES:(g + 1) * LANES], _v_page(v_refs[2 * g], v_refs[2 * g + 1]))
    acc_s[...] = alpha * acc_s[...] + pv
    m_s[...] = m_new

    @pl.when(j == pl.num_programs(1) - 1)
    def _():
        qf = qb[...].astype(F32)
        kn = kn_ref[0].astype(BF16).astype(F32)
        vn = vn_ref[0].astype(BF16).astype(F32)
        s_new = jnp.sum(qf * kn, axis=1, keepdims=True) * scale
        m_fin = jnp.maximum(m_s[...], s_new)
        a = jnp.exp(m_s[...] - m_fin)
        p_new = jnp.exp(s_new - m_fin)
        l_fin = l_s[...] * a + p_new
        accn = (acc_s[...] * a + p_new.astype(BF16).astype(F32) * vn) / l_fin
        lam = lam_ref[0]
        for h in range(H):
            cs = slice(h * VD, (h + 1) * VD)
            o = accn[2 * h:2 * h + 1, cs] - lam * accn[2 * h + 1:2 * h + 2, cs]
            o = o * lax.rsqrt(jnp.mean(o * o, axis=1, keepdims=True) + HEAD_EPS) * gh_ref[...] * (1.0 - LAM_INIT)
            o_ref[0, :, cs] = o


def _decode_call(page_table, lam, q, k_new, v_new, g_head, cache_k, cache_v, *, n_pool, n_pg):
    b, n_pages = page_table.shape
    page = cache_k.shape[0] // n_pool
    row_spec = pl.BlockSpec((1, 1, D), lambda bi, j, pt: (bi, 0, 0))
    page_specs = [pl.BlockSpec((page, 8, LANES),
                               functools.partial(lambda bi, j, pt, g, half: (pt[bi, j * n_pg + g], half, 0), g=g, half=half))
                  for g in range(n_pg) for half in range(2)]
    body = functools.partial(_decode_body, n_pg=n_pg, scale=HD ** -0.5)
    grid_spec = pltpu.PrefetchScalarGridSpec(
        num_scalar_prefetch=1,
        grid=(b, n_pages // n_pg),
        in_specs=[pl.BlockSpec(memory_space=pltpu.SMEM), row_spec, row_spec, row_spec,
                  pl.BlockSpec((1, VD), lambda bi, j, pt: (0, 0))] + page_specs + page_specs,
        out_specs=pl.BlockSpec((1, 1, D), lambda bi, j, pt: (bi, 0, 0)),
        scratch_shapes=[pltpu.VMEM((N_MAPS, D), BF16), pltpu.VMEM((N_MAPS, 1), F32), pltpu.VMEM((N_MAPS, 1), F32),
                        pltpu.VMEM((N_MAPS, D), F32)],
    )
    return pl.pallas_call(
        body,
        grid_spec=grid_spec,
        out_shape=jax.ShapeDtypeStruct((b, 1, D), F32),
        compiler_params=_params(("parallel", "arbitrary"), 56),
        name="decode_attn",
    )(page_table, lam, q, k_new, v_new, g_head, *([cache_k] * (2 * n_pg)), *([cache_v] * (2 * n_pg)))


CONV_HALO = 32
CONV_RC = 32
CONV_CB = 512


def _ln_swish(acc, g, b):
    xc = acc - jnp.mean(acc, axis=1, keepdims=True)
    y = xc * lax.rsqrt(jnp.mean(xc * xc, axis=1, keepdims=True) + EPS) * g + b
    return y * _sigmoid(y)


def _conv_body(cur_ref, prev_ref, w_ref, bdw_ref, g_ref, b_ref, z_ref, full, ph, uc, *, tt):
    t = pl.program_id(1)
    full[0:CONV_HALO, :] = jnp.where(t > 0, prev_ref[0], 0.0).astype(BF16).astype(F32)
    full[CONV_HALO:, :] = cur_ref[0].astype(BF16).astype(F32)
    rows = ph.shape[1]
    for s in range(1, SUBLANES):
        for cb in range(D // CONV_CB):
            cs = slice(cb * CONV_CB, (cb + 1) * CONV_CB)
            ph[s - 1, :, cs] = full[s:s + rows, cs]
    first = CONV_HALO - HIST

    def chunk(c, _):
        r0 = pl.multiple_of(c * CONV_RC, CONV_RC)
        for cb in range(D // CONV_CB):
            cs = slice(cb * CONV_CB, (cb + 1) * CONV_CB)
            accs = [bdw_ref[:, cs] for _ in range(CONV_RC // SUBLANES)]
            for j in range(CONV_W):
                a, s = divmod(first + j, SUBLANES)
                src = full if s == 0 else ph.at[s - 1]
                wj = w_ref[j, :, cs]
                for rg in range(CONV_RC // SUBLANES):
                    accs[rg] = accs[rg] + src[pl.ds(r0 + SUBLANES * (a + rg), SUBLANES), cs] * wj
            for rg in range(CONV_RC // SUBLANES):
                uc[rg * SUBLANES:(rg + 1) * SUBLANES, cs] = accs[rg]
        z_ref[0, pl.ds(r0, CONV_RC), :] = _ln_swish(uc[...], g_ref[...], b_ref[...]).astype(z_ref.dtype)
        return 0

    lax.fori_loop(0, tt // CONV_RC, chunk, 0)


def _conv_call(u, w_dw, b_dw, g_cln, b_cln, *, tt):
    b, s, _ = u.shape
    ratio = tt // CONV_HALO
    vec = pl.BlockSpec((1, D), lambda bi, t: (0, 0))
    return pl.pallas_call(
        functools.partial(_conv_body, tt=tt),
        grid=(b, s // tt),
        in_specs=[pl.BlockSpec((1, tt, D), lambda bi, t: (bi, t, 0)),
                  pl.BlockSpec((1, CONV_HALO, D), lambda bi, t: (bi, jnp.maximum(t * ratio - 1, 0), 0)),
                  pl.BlockSpec((CONV_HALO, SUBLANES, D), lambda bi, t: (0, 0, 0)),
                  pl.BlockSpec((SUBLANES, D), lambda bi, t: (0, 0)), vec, vec],
        out_specs=pl.BlockSpec((1, tt, D), lambda bi, t: (bi, t, 0)),
        out_shape=jax.ShapeDtypeStruct((b, s, D), BF16),
        scratch_shapes=[pltpu.VMEM((tt + CONV_HALO, D), F32),
                        pltpu.VMEM((SUBLANES - 1, tt + CONV_HALO - SUBLANES, D), F32),
                        pltpu.VMEM((CONV_RC, D), F32)],
        compiler_params=_params(("parallel", "arbitrary"), 40),
        name="prompt_conv",
    )(u, u, w_dw, b_dw, g_cln, b_cln)


def _sconv_body(uv_ref, ug_ref, st_ref, w_ref, bdw_ref, g_ref, b_ref, u_ref, z_ref):
    u = uv_ref[...] * _sigmoid(ug_ref[...])
    rnd = lambda a: a.astype(BF16).astype(F32)
    acc = bdw_ref[...] + rnd(u) * w_ref[HIST:HIST + 1, :]
    for j in range(HIST):
        acc = acc + rnd(st_ref[j]) * w_ref[j:j + 1, :]
    u_ref[...] = u
    z_ref[...] = _ln_swish(acc, g_ref[...], b_ref[...])


def _sconv_call(u_val, u_gate, state_t, w_dw, b_dw, g_cln, b_cln):
    b = u_val.shape[0]
    return pl.pallas_call(
        _sconv_body,
        out_shape=[jax.ShapeDtypeStruct((b, D), F32), jax.ShapeDtypeStruct((b, D), F32)],
        compiler_params=pltpu.CompilerParams(vmem_limit_bytes=40 * MIB),
        name="sample_conv",
    )(u_val, u_gate, state_t, w_dw, b_dw, g_cln, b_cln)


def _router_body(xp_ref, xs_ref, g_ref, scp_ref, shp_ref, scs_ref, shs_ref, wr_ref, br_ref,
                 h2_ref, rid_ref, rwt_ref, *, n_prompt_tiles, n_sample):
    tm = xp_ref.shape[0]
    is_s = pl.program_id(0) == n_prompt_tiles
    x = jnp.where(is_s, xs_ref[...], xp_ref[...])
    sc = jnp.where(is_s, scs_ref[0], scp_ref[0])
    sh = jnp.where(is_s, shs_ref[0], shp_ref[0])
    n_valid = jnp.where(is_s, n_sample, tm)
    h2 = _rms(x, g_ref[...]) * (1.0 + sc) + sh
    h2_ref[...] = h2
    logits = _dot(h2, wr_ref[...]) + br_ref[...]
    lane = lax.broadcasted_iota(I32, (tm, LANES), 1).astype(F32)
    far = float(LANES)
    gl = jnp.where(lane < N_GROUPS, logits, NEG)
    gmax = jnp.max(gl, axis=1, keepdims=True)
    grp = jnp.min(jnp.where(gl == gmax, lane, far), axis=1, keepdims=True)
    p_grp = 1.0 / jnp.sum(jnp.where(lane < N_GROUPS, jnp.exp(gl - gmax), 0.0), axis=1, keepdims=True)
    lo = N_GROUPS + grp * EPG
    el = jnp.where((lane >= lo) & (lane < lo + EPG), logits, NEG)
    t1 = jnp.max(el, axis=1, keepdims=True)
    i1 = jnp.min(jnp.where(el == t1, lane, far), axis=1, keepdims=True)
    el2 = jnp.where(lane == i1, NEG, el)
    t2 = jnp.max(el2, axis=1, keepdims=True)
    i2 = jnp.min(jnp.where(el2 == t2, lane, far), axis=1, keepdims=True)
    e = jnp.exp(t2 - t1)
    w1 = p_grp / (1.0 + e)
    w2 = p_grp * e / (1.0 + e)
    valid = lax.broadcasted_iota(I32, (tm, LANES), 0) < n_valid
    ids = jnp.where(lane == 0.0, i1 - N_GROUPS, jnp.where(lane == 1.0, i2 - N_GROUPS, -1.0))
    rid_ref[...] = jnp.where(valid, ids, -1.0).astype(I32)
    rwt_ref[...] = jnp.where(lane == 0.0, w1, jnp.where(lane == 1.0, w2, 0.0))


def _router_call(x_p, x_s, g, sc_p, sh_p, sc_s, sh_s, wr, br, *, seq, n_sample):
    tm = TOK_TILE
    npt = x_p.shape[0] // tm
    n_total = x_p.shape[0] + tm
    prow = lambda i: jnp.minimum(i, npt - 1)
    pmod = pl.BlockSpec((1, 1, D), lambda i: (prow(i) * tm // seq, 0, 0))
    smod = pl.BlockSpec((1, tm, D), lambda i: (0, 0, 0))
    return pl.pallas_call(
        functools.partial(_router_body, n_prompt_tiles=npt, n_sample=n_sample),
        grid=(npt + 1,),
        in_specs=[pl.BlockSpec((tm, D), lambda i: (prow(i), 0)), pl.BlockSpec((tm, D), lambda i: (0, 0)),
                  pl.BlockSpec((1, D), lambda i: (0, 0)), pmod, pmod, smod, smod,
                  pl.BlockSpec((D, LANES), lambda i: (0, 0)), pl.BlockSpec((1, LANES), lambda i: (0, 0))],
        out_specs=[pl.BlockSpec((tm, D), lambda i: (i, 0)), pl.BlockSpec((tm, LANES), lambda i: (i, 0)),
                   pl.BlockSpec((tm, LANES), lambda i: (i, 0))],
        out_shape=[jax.ShapeDtypeStruct((n_total, D), F32), jax.ShapeDtypeStruct((n_total, LANES), I32),
                   jax.ShapeDtypeStruct((n_total, LANES), F32)],
        compiler_params=_params(("arbitrary",), 40),
        name="router",
    )(x_p, x_s, g, sc_p, sh_p, sc_s, sh_s, wr, br)


def _rank_body(rid_ref, rank_ref, cnt_ref, carry):
    i = pl.program_id(0)
    tt = rid_ref.shape[0]

    @pl.when(i == 0)
    def _():
        carry[...] = jnp.zeros((1, LANES), F32)

    ids = rid_ref[...]
    lane = lax.broadcasted_iota(I32, (tt, LANES), 1)
    e1 = ids[:, 0:1]
    e2 = ids[:, 1:2]
    hit = jnp.where((lane == e1) | (lane == e2), 1.0, 0.0)
    row = lax.broadcasted_iota(I32, (tt, tt), 0)
    col = lax.broadcasted_iota(I32, (tt, tt), 1)
    before = jnp.where(col < row, 1.0, 0.0).astype(BF16)
    cum = _dot(before, hit.astype(BF16)) + carry[...]
    r1 = jnp.sum(jnp.where(lane == e1, cum, 0.0), axis=1, keepdims=True)
    r2 = jnp.sum(jnp.where(lane == e2, cum, 0.0), axis=1, keepdims=True)
    rank_ref[...] = jnp.where(lane == 0, r1, jnp.where(lane == 1, r2, 0.0)).astype(I32)
    total = carry[...] + jnp.sum(hit, axis=0, keepdims=True)
    carry[...] = total
    cnt_ref[...] = jnp.broadcast_to(total, cnt_ref.shape)


def _rank_call(rid):
    n = rid.shape[0]
    return pl.pallas_call(
        _rank_body,
        grid=(n // TOK_TILE,),
        in_specs=[pl.BlockSpec((TOK_TILE, LANES), lambda i: (i, 0))],
        out_specs=[pl.BlockSpec((TOK_TILE, LANES), lambda i: (i, 0)), pl.BlockSpec((8, LANES), lambda i: (0, 0))],
        out_shape=[jax.ShapeDtypeStruct((n, LANES), I32), jax.ShapeDtypeStruct((8, LANES), F32)],
        scratch_shapes=[pltpu.VMEM((1, LANES), F32)],
        compiler_params=_params(("arbitrary",), 32),
        name="moe_rank",
    )(rid)


def _row_copy(src_hbm, row, dst, slot, r, sem):
    return pltpu.make_async_copy(src_hbm.at[pl.ds(row, 1)], dst.at[slot, pl.ds(r, 1)], sem.at[slot])


def _gather_start(idx_ref, src_hbm, dst, slot, sem, n_rows):
    for r in range(n_rows):
        _row_copy(src_hbm, idx_ref[0, 0, r], dst, slot, r, sem).start()


def _gather_wait(src_hbm, dst, slot, sem, n_rows):
    def body(r, _):
        _row_copy(src_hbm, 0, dst, slot, r, sem).wait()
        return 0
    lax.fori_loop(0, n_rows, body, 0, unroll=8)


def _expert_body(be_ref, nb_ref, idx_cur, idx_nxt, h_hbm, wg_ref, wu_ref, wd_ref, y_ref, xs, sem):
    i = pl.program_id(0)
    nb = nb_ref[0]
    slot = i % 2
    rows = xs.shape[1]

    @pl.when(i == 0)
    def _():
        _gather_start(idx_cur, h_hbm, xs, 0, sem, rows)

    @pl.when(i + 1 < nb)
    def _():
        _gather_start(idx_nxt, h_hbm, xs, 1 - slot, sem, rows)

    @pl.when(i < nb)
    def _():
        _gather_wait(h_hbm, xs, slot, sem, rows)
        x = xs[slot].astype(BF16)
        g = _dot(x, wg_ref[0])
        u = _dot(x, wu_ref[0])
        y_ref[...] = _dot((g * _sigmoid(g) * u).astype(BF16), wd_ref[0])

    @pl.when(i >= nb)
    def _():
        y_ref[...] = jnp.zeros(y_ref.shape, F32)


def _expert_call(block_expert, n_used, slot_token, h2, wg, wu, wd):
    n_blocks = slot_token.shape[0]
    idx_spec = lambda f: pl.BlockSpec((1, 1, MOE_ROWS), f, memory_space=pltpu.SMEM)
    grid_spec = pltpu.PrefetchScalarGridSpec(
        num_scalar_prefetch=2,
        grid=(n_blocks,),
        in_specs=[idx_spec(lambda i, be, nb: (i, 0, 0)),
                  idx_spec(lambda i, be, nb: (jnp.minimum(i + 1, n_blocks - 1), 0, 0)),
                  pl.BlockSpec(memory_space=pl.ANY),
                  pl.BlockSpec((1, D, D_EXPERT), lambda i, be, nb: (be[i], 0, 0)),
                  pl.BlockSpec((1, D, D_EXPERT), lambda i, be, nb: (be[i], 0, 0)),
                  pl.BlockSpec((1, D_EXPERT, D), lambda i, be, nb: (be[i], 0, 0))],
        out_specs=pl.BlockSpec((MOE_ROWS, D), lambda i, be, nb: (i, 0)),
        scratch_shapes=[pltpu.VMEM((2, MOE_ROWS, D), F32), pltpu.SemaphoreType.DMA((2,))],
    )
    return pl.pallas_call(
        _expert_body,
        grid_spec=grid_spec,
        out_shape=jax.ShapeDtypeStruct((n_blocks * MOE_ROWS, D), F32),
        compiler_params=_params(("arbitrary",), 56),
        name="moe_experts",
    )(block_expert, n_used, slot_token, slot_token, h2, wg, wu, wd)


def _combine_body(d_cur, d_nxt, ys_hbm, x_ref, g2_ref, wt_ref, gf_ref, o_ref, buf, sem, *, n_tiles):
    i = pl.program_id(0)
    slot = i % 2
    tt = x_ref.shape[0]

    @pl.when(i == 0)
    def _():
        _gather_start(d_cur, ys_hbm, buf, 0, sem, 2 * tt)

    @pl.when(i + 1 < n_tiles)
    def _():
        _gather_start(d_nxt, ys_hbm, buf, 1 - slot, sem, 2 * tt)

    _gather_wait(ys_hbm, buf, slot, sem, 2 * tt)
    wt = wt_ref[...]
    ffn = wt[:, 0:1] * buf[slot, 0:tt, :] + wt[:, 1:2] * buf[slot, tt:2 * tt, :]
    x = x_ref[...] + g2_ref[0] * ffn
    o_ref[...] = _rms(x, gf_ref[...])


def _combine_call(dest_tiles, ys, x, g2, rwt, g_final, *, rows_per_mod, tile_off):
    tt = COMB_TILE
    m = x.shape[0]
    n_tiles = m // tt
    r = g2.shape[1]
    idx_spec = lambda f: pl.BlockSpec((1, 1, 2 * tt), f, memory_space=pltpu.SMEM)
    return pl.pallas_call(
        functools.partial(_combine_body, n_tiles=n_tiles),
        grid=(n_tiles,),
        in_specs=[idx_spec(lambda i: (i + tile_off, 0, 0)),
                  idx_spec(lambda i: (jnp.minimum(i + 1, n_tiles - 1) + tile_off, 0, 0)),
                  pl.BlockSpec(memory_space=pl.ANY),
                  pl.BlockSpec((tt, D), lambda i: (i, 0)),
                  pl.BlockSpec((1, r, D), lambda i: (i * tt // rows_per_mod, 0, 0)),
                  pl.BlockSpec((tt, LANES), lambda i: (i + tile_off, 0)),
                  pl.BlockSpec((1, D), lambda i: (0, 0))],
        out_specs=pl.BlockSpec((tt, D), lambda i: (i, 0)),
        out_shape=jax.ShapeDtypeStruct((m, D), F32),
        scratch_shapes=[pltpu.VMEM((2, 2 * tt, D), F32), pltpu.SemaphoreType.DMA((2,))],
        compiler_params=_params(("arbitrary",), 40),
        name="moe_combine",
    )(dest_tiles, dest_tiles, ys, x, g2, rwt, g_final)


def kernel(x_prompt, x_sample, cache_k, cache_v, state_conv, page_table, c_prompt, c_sample, w_ada, b_ada, g_mix, w_in, lambda_q1, lambda_k1, lambda_q2, lambda_k2, g_head, w_attn_proj, w_dw, b_dw, g_conv_ln, b_conv_ln, w_conv_proj, w_out, g_ffn, w_group_router, b_group_router, w_expert_router, b_expert_router, w_gate_e, w_up_e, w_down_e, g_final):
    assert w_in.shape[0] == 1, "single-layer step"
    bp, seq, _ = x_prompt.shape
    bs = x_sample.shape[0]
    assert x_sample.shape[1] == 1
    tp = bp * seq
    n_pool, page = cache_k.shape[1], cache_k.shape[2]

    row = lambda a: a.reshape(1, -1)
    lam = (jnp.exp(jnp.sum(lambda_q1[0] * lambda_k1[0])) - jnp.exp(jnp.sum(lambda_q2[0] * lambda_k2[0]))
           + LAM_INIT).reshape(1).astype(F32)
    xp = x_prompt.reshape(tp, D)
    xs = x_sample.reshape(bs, D)

    c_all = jnp.concatenate([c_prompt, c_sample], axis=0)
    (mod,) = _mm_call([c_all], [(w_ada[0], 0)], [(0, 0)],
                      [(row(b_ada[0]), (1, 512), lambda i, j: (0, j))],
                      lambda accs, ex: [accs[0] + ex[0]], [F32], name="adaln_mod", tm=c_all.shape[0], tn=512, n_col=N_MOD * D // 512,
                      prologue=lambda lhs, ex: [lhs[0] * _sigmoid(lhs[0])])
    mod_p = [mod[:bp, k * D:(k + 1) * D].reshape(bp, 1, D) for k in range(N_MOD)]
    mod_s = [mod[bp:, k * D:(k + 1) * D] for k in range(N_MOD)]

    w_in_b = w_in[0].astype(BF16)
    w_attn_b = w_attn_proj[0].astype(BF16)
    w_conv_b = w_conv_proj[0].astype(BF16)
    w_out_b = w_out[0].astype(BF16)

    full = lambda i, j: (0, 0)
    (proj_s,) = _mm_call([xs], [(w_in_b, 0)], [(0, 0)],
                         [(row(g_mix[0]), (1, D), full), (mod_s[1], (bs, D), full), (mod_s[0], (bs, D), full)],
                         lambda accs, ex: [accs[0]], [F32], name="sample_inproj", tm=bs, tn=512, n_col=N_IN // 512,
                         prologue=lambda lhs, ex: [_rms(lhs[0], ex[0]) * (1.0 + ex[1]) + ex[2]])
    seg = lambda k: proj_s[:, k * D:(k + 1) * D]
    q_s, k_s, v_s, uval_s, ugate_s, agate_s, cgate_s = (seg(k) for k in range(7))
    o_s = _decode_call(page_table, lam, q_s.reshape(bs, 1, D), k_s.reshape(bs, 1, D), v_s.reshape(bs, 1, D),
                       row(g_head[0]), cache_k.reshape(n_pool * page, N_MAPS, HD),
                       cache_v.reshape(n_pool * page, H, 2, LANES).transpose(0, 2, 1, 3).reshape(n_pool * page, 2 * H, LANES),
                       n_pool=n_pool, n_pg=4)
    ident = lambda accs, ex: [accs[0]]
    (yattn_s,) = _mm_call([o_s.reshape(bs, D)], [(w_attn_b, 0)], [(0, 0)], [], ident, [F32],
                          name="sample_attnproj", tm=bs, tn=512, n_col=D // 512)
    w_dw_pad = jnp.pad(w_dw[0], ((0, CONV_HALO - CONV_W), (0, 0)))
    u_s, z_s = _sconv_call(uval_s, ugate_s, jnp.swapaxes(state_conv[0], 0, 1), w_dw_pad, row(b_dw[0]),
                           row(g_conv_ln[0]), row(b_conv_ln[0]))
    (yconv_s,) = _mm_call([z_s], [(w_conv_b, 0)], [(0, 0)], [], ident, [F32],
                          name="sample_convproj", tm=bs, tn=512, n_col=D // 512)
    tile_j = lambda i, j: (0, j)
    (x1_s,) = _mm_call([yattn_s, yconv_s], [(w_out_b, 0)], [(0, 0)],
                       [(agate_s, (bs, D), full), (cgate_s, (bs, D), full), (xs, (bs, 512), tile_j),
                        (mod_s[2], (bs, 512), tile_j)],
                       lambda accs, ex: [ex[2] + ex[3] * accs[0]], [F32], name="sample_outproj", tm=bs, tn=512, n_col=D // 512,
                       prologue=lambda lhs, ex: [_sigmoid(ex[0]) * lhs[0] + _sigmoid(ex[1]) * lhs[1]])

    h_p = _normmod_call(xp, row(g_mix[0]), mod_p[1], mod_p[0], tm=512, rows_per_mod=seq, out_dtype=BF16)
    mm = functools.partial(_mm_call, [h_p])
    col = D // 1024
    (q_p,) = mm([(w_in_b, 0)], [(0, 0)], [], ident, [BF16], name="prompt_q", tm=1024, tn=1024, n_col=col)
    k_p = _kproj_call(h_p, w_in_b, col, tm=1024)
    (v_p,) = mm([(w_in_b, 2 * col)], [(0, 0)], [], ident, [F32], name="prompt_v", tm=1024, tn=1024, n_col=col)
    col5 = D // 512
    u_p, w_up_b = mm([(w_in_b, 3 * col5), (w_in_b, 4 * col5)], [(0, 0), (0, 1)], [],
                     lambda accs, ex: [accs[0] * _sigmoid(accs[1])], [F32], name="prompt_glu", tm=1024, tn=512,
                     n_col=col5, side_casts=(w_up_e[0].reshape(N_EXPERTS * D, D_EXPERT),))
    gates_p, w_gate_b = mm([(w_in_b, 5 * col)], [(0, 0)], [], lambda accs, ex: [_sigmoid(accs[0])], [BF16],
                           name="prompt_gates", tm=1024, tn=1024, n_col=2 * col,
                           side_casts=(w_gate_e[0].reshape(N_EXPERTS * D, D_EXPERT),))
    o_p = _attn_call(lam, q_p, k_p, v_p, row(g_head[0]), batch=bp, seq=seq, tq=512, hpb=2)
    z_p = _conv_call(u_p.reshape(bp, seq, D), jnp.broadcast_to(w_dw_pad[:, None, :], (CONV_HALO, SUBLANES, D)),
                     jnp.broadcast_to(row(b_dw[0]), (SUBLANES, D)), row(g_conv_ln[0]), row(b_conv_ln[0]),
                     tt=256).reshape(tp, D)
    merged_p, w_down_b = _mm_call([o_p, z_p], [(w_attn_b, 0), (w_conv_b, 0)],
                                  [(0, 0), (1, 1)],
                                  [(gates_p, (1024, 512), lambda i, j: (i, j)),
                                   (gates_p, (1024, 512), lambda i, j: (i, j + col5))],
                                  lambda accs, ex: [ex[0].astype(F32) * accs[0] + ex[1].astype(F32) * accs[1]], [BF16],
                                  name="prompt_merge", tm=1024, tn=512, n_col=col5,
                                  side_casts=(w_down_e[0].reshape(N_EXPERTS * D_EXPERT, D),))
    (x1_p,) = _mm_call([merged_p], [(w_out_b, 0)], [(0, 0)],
                       [(xp, (1024, 1024), lambda i, j: (i, j)),
                        (mod_p[2], (1, 1, 1024), lambda i, j: (i * 1024 // seq, 0, j))],
                       lambda accs, ex: [ex[0] + ex[1][0] * accs[0]], [F32], name="prompt_outproj", tm=1024, tn=1024,
                       n_col=col)

    n_joint = tp + TOK_TILE
    wr = jnp.pad(jnp.concatenate([w_group_router[0], w_expert_router[0]], axis=1),
                 ((0, 0), (0, LANES - N_GROUPS - N_EXPERTS)))
    br = jnp.pad(jnp.concatenate([b_group_router[0], b_expert_router[0]]), (0, LANES - N_GROUPS - N_EXPERTS))
    pad_rows = lambda a, n: jnp.pad(a, ((0, n - a.shape[0]), (0, 0)))
    h2, rid, rwt = _router_call(x1_p, pad_rows(x1_s, TOK_TILE), row(g_ffn[0]), mod_p[4], mod_p[3],
                                pad_rows(mod_s[4], TOK_TILE)[None], pad_rows(mod_s[3], TOK_TILE)[None], wr, row(br),
                                seq=seq, n_sample=bs)
    rank, cnt = _rank_call(rid)

    n_pairs = 2 * (tp + bs)
    n_blocks = -(-n_pairs // MOE_ROWS) + N_EXPERTS
    counts = cnt[0, :N_EXPERTS].astype(I32)
    padded = (counts + MOE_ROWS - 1) // MOE_ROWS * MOE_ROWS
    pad_end = jnp.cumsum(padded)
    pad_start = pad_end - padded
    n_used = (pad_end[-1] // MOE_ROWS).astype(I32)
    expert = rid[:, :2]
    live = expert >= 0
    dest = jnp.where(live, pad_start[jnp.clip(expert, 0, N_EXPERTS - 1)] + rank[:, :2], 0)
    token = jnp.broadcast_to(jnp.arange(n_joint, dtype=I32)[:, None], dest.shape)
    slot_token = jnp.zeros((n_blocks * MOE_ROWS,), I32).at[jnp.where(live, dest, n_blocks * MOE_ROWS)].set(
        token, mode="drop")
    blk = jnp.arange(n_blocks, dtype=I32)
    first_row = jnp.minimum(blk, n_used - 1) * MOE_ROWS
    block_expert = jnp.minimum(jnp.sum((pad_end[None, :] <= first_row[:, None]).astype(I32), axis=1), N_EXPERTS - 1)
    ys = _expert_call(block_expert, n_used.reshape(1), slot_token.reshape(n_blocks, 1, MOE_ROWS), h2,
                      w_gate_b.reshape(N_EXPERTS, D, D_EXPERT), w_up_b.reshape(N_EXPERTS, D, D_EXPERT),
                      w_down_b.reshape(N_EXPERTS, D_EXPERT, D))

    dest_tiles = dest.reshape(n_joint // COMB_TILE, COMB_TILE, 2).transpose(0, 2, 1).reshape(
        n_joint // COMB_TILE, 1, 2 * COMB_TILE)
    y_p = _combine_call(dest_tiles, ys, x1_p, mod_p[5], rwt, row(g_final), rows_per_mod=seq, tile_off=0)
    y_s = _combine_call(dest_tiles, ys, pad_rows(x1_s, COMB_TILE), pad_rows(mod_s[5], COMB_TILE)[None], rwt,
                        row(g_final), rows_per_mod=COMB_TILE, tile_off=tp // COMB_TILE)[:bs]

    conv_p = u_p.reshape(bp, seq, D)[:, seq - HIST:, :]
    conv_s = jnp.concatenate([state_conv[0][:, 1:, :], u_s[:, None, :]], axis=1)
    return (y_p.reshape(bp, seq, D), y_s.reshape(bs, 1, D),
            k_p.reshape(1, bp, seq, H, 2, HD), v_p.reshape(1, bp, seq, H, VD), conv_p[None],
            k_s.reshape(1, bs, 1, H, 2, HD), v_s.reshape(1, bs, 1, H, VD), conv_s[None])
```
